```python
import math
import jax
import jax.numpy as jnp
from jax import lax
import numpy as np

D_MODEL = 1024
BATCH = 16
SEQ = 4096
DEPTH = 1
DEC_BATCH = 16
DEC_SEQ = 2048
PAST_LEN = 128

N_META = 16
D_FF = 2816
EPS = 1e-6
H_A = 8
DH_A = 64
DV_A = 2 * DH_A
Q_BLOCK = 128
N_BUCKETS = 32
MAX_DISTANCE = 128
H_R = 16
N_R = 64
C_R = H_R * N_R
R_W = 64
R_A = 64
R_G = 128
LNX_EPS = 64e-5
ATT_QK = H_A * 2 * DH_A
ATT_V = H_A * DV_A
ATT_COLS = 2 * ATT_QK + ATT_V
RW_SIZES = (C_R, C_R, C_R, R_W, R_W, R_A, R_A, R_G)
RW_COLS = sum(RW_SIZES)
GATE_COLS = 2 * D_MODEL
N_IN = ATT_COLS + RW_COLS + GATE_COLS

kernel_name = 'hybrid_diffattn_rwkv7_encoder'


def _split(x, sizes):
    cuts, acc = [], 0
    for s in sizes[:-1]:
        acc += s
        cuts.append(acc)
    return jnp.split(x, cuts, axis=-1)


def _rmsnorm(x, g):
    xf = x.astype(jnp.float32)
    y = xf * lax.rsqrt(jnp.mean(xf * xf, axis=-1, keepdims=True) + EPS)
    return (y * g.astype(jnp.float32)).astype(x.dtype)


def _swiglu(x, w_gate, w_up, w_down):
    return (jax.nn.silu(x @ w_gate) * (x @ w_up)) @ w_down


def _rel_bucket(rel):
    nb = N_BUCKETS // 2
    max_exact = nb // 2
    n = jnp.abs(rel)
    nf = jnp.maximum(n, 1).astype(jnp.float32)
    large = max_exact + (jnp.log(nf / max_exact) / math.log(MAX_DISTANCE / max_exact) * (nb - max_exact)).astype(jnp.int32)
    large = jnp.minimum(large, nb - 1)
    return (rel > 0).astype(jnp.int32) * nb + jnp.where(n < max_exact, n, large)


def _diff_attention(q, k, v, lam, rel_bias):
    B, L = q.shape[0], q.shape[1]
    n_blk = -(-L // Q_BLOCK)
    Lp = n_blk * Q_BLOCK
    qb = jnp.pad(q, ((0, 0), (0, Lp - L), (0, 0), (0, 0), (0, 0)))
    qb = jnp.moveaxis(qb.reshape(B, n_blk, Q_BLOCK, H_A, 2, DH_A), 1, 0)
    k1, k2 = k[..., 0, :], k[..., 1, :]
    k_pos = jnp.arange(L, dtype=jnp.int32)
    scale = DH_A ** -0.5
    table = rel_bias.astype(jnp.float32)

    def block(args):
        q_blk, start = args
        q_pos = start + jnp.arange(Q_BLOCK, dtype=jnp.int32)
        bias = jnp.transpose(table[_rel_bucket(k_pos[None, :] - q_pos[:, None])], (2, 0, 1))
        s1 = jnp.einsum('bqhd,bkhd->bhqk', q_blk[..., 0, :], k1) * scale + bias
        s2 = jnp.einsum('bqhd,bkhd->bhqk', q_blk[..., 1, :], k2) * scale + bias
        p = jax.nn.softmax(s1, axis=-1) - lam * jax.nn.softmax(s2, axis=-1)
        return jnp.einsum('bhqk,bkhd->bqhd', p, v)

    starts = jnp.arange(n_blk, dtype=jnp.int32) * Q_BLOCK
    out = lax.map(block, (qb, starts))
    return jnp.moveaxis(out, 0, 1).reshape(B, Lp, H_A, DV_A)[:, :L]


def _diff_attn_branch(p, l, P):
    B, L = p.shape[0], p.shape[1]
    q, k, v = _split(p.astype(jnp.float32), (ATT_QK, ATT_QK, ATT_V))
    q = q.reshape(B, L, H_A, 2, DH_A)
    k = k.reshape(B, L, H_A, 2, DH_A)
    v = v.reshape(B, L, H_A, DV_A)
    lam_init = 0.8 - 0.6 * math.exp(-0.3 * l)
    lam = (jnp.exp(jnp.sum(P['attn_lambda_q1'][l].astype(jnp.float32) * P['attn_lambda_k1'][l]))
           - jnp.exp(jnp.sum(P['attn_lambda_q2'][l].astype(jnp.float32) * P['attn_lambda_k2'][l])) + lam_init)
    o = _diff_attention(q, k, v, lam, P['rel_bias'])
    o = _rmsnorm(o, P['attn_subln'][l]) * (1.0 - lam_init)
    return o.reshape(B, L, ATT_V) @ P['w_attn_branch'][l]


def _centred_shift_mix(p, mu_prev, mu_next):
    prev = jnp.pad(p, ((0, 0), (1, 0), (0, 0)))[:, :-1]
    nxt = jnp.pad(p, ((0, 0), (0, 1), (0, 0)))[:, 1:]
    return p + mu_prev * (prev - p) + mu_next * (nxt - p)


def _wkv7_scan(r, w, k, v, a, b, reverse):
    def step(S, inp):
        r_t, w_t, k_t, v_t, a_t, b_t = inp
        sa = jnp.einsum('bhvk,bhk->bhv', S, a_t)
        S = S * w_t[:, :, None, :] + sa[..., None] * b_t[:, :, None, :] + v_t[..., None] * k_t[:, :, None, :]
        return S, jnp.einsum('bhvk,bhk->bhv', S, r_t)

    xs = tuple(jnp.moveaxis(t, 1, 0) for t in (r, w, k, v, a, b))
    S0 = jnp.zeros((r.shape[0], H_R, N_R, N_R), jnp.float32)
    _, ys = lax.scan(step, S0, xs, reverse=reverse)
    return jnp.moveaxis(ys, 0, 1)


def _rwkv7_direction(rh, vh, k, kk, wd, ad, w0, w2, a0, a2, k_a, r_k, reverse):
    B, L = k.shape[0], k.shape[1]
    logw = -jax.nn.softplus(-(w0 + jnp.tanh(wd) @ w2)) - 0.5
    decay = jnp.exp(-jnp.exp(logw)).reshape(B, L, H_R, N_R)
    a = jax.nn.sigmoid(a0 + ad @ a2)
    kd = (k * (1.0 + (a - 1.0) * k_a)).reshape(B, L, H_R, N_R)
    ah = a.reshape(B, L, H_R, N_R)
    y = _wkv7_scan(rh, decay, kd, vh, -kk, kk * ah, reverse)
    bonus = jnp.sum(rh * kd * r_k, axis=-1, keepdims=True) * vh
    return y, bonus


def _rwkv7_branch(p, l, P):
    B, L = p.shape[0], p.shape[1]
    p = _centred_shift_mix(p.astype(jnp.float32), P['rw_mu_prev'][l], P['rw_mu_next'][l])
    r, k, v, wd_f, wd_b, ad_f, ad_b, gd = _split(p, RW_SIZES)
    g = jax.nn.sigmoid(gd) @ P['rw_g2'][l]
    kk = (k * P['rw_k_k'][l]).reshape(B, L, H_R, N_R)
    kk = kk / jnp.maximum(jnp.sqrt(jnp.sum(kk * kk, axis=-1, keepdims=True)), 1e-12)
    rh = r.reshape(B, L, H_R, N_R)
    vh = v.reshape(B, L, H_R, N_R)
    y_f, bonus_f = _rwkv7_direction(rh, vh, k, kk, wd_f, ad_f, P['rw_w0'][l, 0], P['rw_w2'][l, 0],
                                    P['rw_a0'][l, 0], P['rw_a2'][l, 0], P['rw_k_a'][l], P['rw_r_k'][l], False)
    y_b, bonus_b = _rwkv7_direction(rh, vh, k, kk, wd_b, ad_b, P['rw_w0'][l, 1], P['rw_w2'][l, 1],
                                    P['rw_a0'][l, 1], P['rw_a2'][l, 1], P['rw_k_a'][l], P['rw_r_k'][l], True)
    y = y_f + y_b
    mu = jnp.mean(y, axis=-1, keepdims=True)
    var = jnp.mean(jnp.square(y - mu), axis=-1, keepdims=True)
    y = ((y - mu) * lax.rsqrt(var + LNX_EPS)).reshape(B, L, C_R) * P['rw_lnx_w'][l] + P['rw_lnx_b'][l]
    y = y + (bonus_f + bonus_b).reshape(B, L, C_R)
    return (y * g) @ P['w_rw_branch'][l]


def _layer(x, l, P):
    dt = x.dtype
    h = x + (0.5 * _swiglu(_rmsnorm(x, P['ffn1_norm'][l]), P['ffn1_w_gate'][l], P['ffn1_w_up'][l], P['ffn1_w_down'][l])).astype(dt)
    u = _rmsnorm(h, P['mix_norm'][l])
    p_att, p_rw, p_gate = _split(u @ P['w_in'][l], (ATT_COLS, RW_COLS, GATE_COLS))
    y_att = _diff_attn_branch(p_att, l, P)
    y_rw = _rwkv7_branch(p_rw, l, P)
    g_att, g_rw = jnp.split(jax.nn.sigmoid(p_gate.astype(jnp.float32)), 2, axis=-1)
    merged = g_att * y_att + g_rw * y_rw
    h = h + (merged @ P['w_out'][l]).astype(dt)
    h = h + (0.5 * _swiglu(_rmsnorm(h, P['ffn2_norm'][l]), P['ffn2_w_gate'][l], P['ffn2_w_up'][l], P['ffn2_w_down'][l])).astype(dt)
    return h


def _encode(x, P):
    B = x.shape[0]
    meta = jnp.broadcast_to(P['meta_tokens'].astype(x.dtype)[None], (B, N_META, D_MODEL))
    h = jnp.concatenate([meta, x], axis=1)
    for l in range(DEPTH):
        h = _layer(h, l, P)
    return _rmsnorm(h, P['final_norm'])[:, N_META:]


def setup_inputs(seed: int = 0) -> dict:
    key = jax.random.key(seed)
    ks = iter(jax.random.split(key, 48))
    f32 = jnp.float32

    def nrm(shape, scale):
        return jax.random.normal(next(ks), shape, f32) * scale

    def gain(shape):
        return 1.0 + nrm(shape, 0.02)

    def unif(shape, lo, hi):
        return jax.random.uniform(next(ks), shape, f32, lo, hi)

    return {
        'x_prompt': nrm((BATCH, SEQ, D_MODEL), 1.0),
        'x_sample': nrm((DEC_BATCH, DEC_SEQ, D_MODEL), 1.0),
        'meta_tokens': nrm((N_META, D_MODEL), 1.0),
        'rel_bias': nrm((N_BUCKETS, H_A), 0.5),
        'ffn1_norm': gain((DEPTH, D_MODEL)),
        'ffn1_w_gate': nrm((DEPTH, D_MODEL, D_FF), D_MODEL ** -0.5),
        'ffn1_w_up': nrm((DEPTH, D_MODEL, D_FF), D_MODEL ** -0.5),
        'ffn1_w_down': nrm((DEPTH, D_FF, D_MODEL), D_FF ** -0.5),
        'mix_norm': gain((DEPTH, D_MODEL)),
        'w_in': nrm((DEPTH, D_MODEL, N_IN), D_MODEL ** -0.5),
        'attn_lambda_q1': nrm((DEPTH, DH_A), 0.1),
        'attn_lambda_k1': nrm((DEPTH, DH_A), 0.1),
        'attn_lambda_q2': nrm((DEPTH, DH_A), 0.1),
        'attn_lambda_k2': nrm((DEPTH, DH_A), 0.1),
        'attn_subln': gain((DEPTH, DV_A)),
        'w_attn_branch': nrm((DEPTH, ATT_V, D_MODEL), ATT_V ** -0.5),
        'rw_mu_prev': unif((DEPTH, RW_COLS), 0.0, 0.5),
        'rw_mu_next': unif((DEPTH, RW_COLS), 0.0, 0.5),
        'rw_w0': unif((DEPTH, 2, C_R), -6.0, -1.0),
        'rw_w2': nrm((DEPTH, 2, R_W, C_R), 0.5 * R_W ** -0.5),
        'rw_a0': nrm((DEPTH, 2, C_R), 0.5),
        'rw_a2': nrm((DEPTH, 2, R_A, C_R), 0.5 * R_A ** -0.5),
        'rw_g2': nrm((DEPTH, R_G, C_R), R_G ** -0.5),
        'rw_k_k': 0.85 + nrm((DEPTH, C_R), 0.02),
        'rw_k_a': gain((DEPTH, C_R)),
        'rw_r_k': nrm((DEPTH, H_R, N_R), 0.1),
        'rw_lnx_w': gain((DEPTH, C_R)),
        'rw_lnx_b': nrm((DEPTH, C_R), 0.02),
        'w_rw_branch': nrm((DEPTH, C_R, D_MODEL), C_R ** -0.5),
        'w_out': nrm((DEPTH, D_MODEL, D_MODEL), D_MODEL ** -0.5),
        'ffn2_norm': gain((DEPTH, D_MODEL)),
        'ffn2_w_gate': nrm((DEPTH, D_MODEL, D_FF), D_MODEL ** -0.5),
        'ffn2_w_up': nrm((DEPTH, D_MODEL, D_FF), D_MODEL ** -0.5),
        'ffn2_w_down': nrm((DEPTH, D_FF, D_MODEL), D_FF ** -0.5),
        'final_norm': gain((D_MODEL,)),
    }


def reference(x_prompt, x_sample, meta_tokens, rel_bias, ffn1_norm, ffn1_w_gate, ffn1_w_up, ffn1_w_down,
              mix_norm, w_in, attn_lambda_q1, attn_lambda_k1, attn_lambda_q2, attn_lambda_k2, attn_subln,
              w_attn_branch, rw_mu_prev, rw_mu_next, rw_w0, rw_w2, rw_a0, rw_a2, rw_g2, rw_k_k, rw_k_a,
              rw_r_k, rw_lnx_w, rw_lnx_b, w_rw_branch, w_out, ffn2_norm, ffn2_w_gate, ffn2_w_up,
              ffn2_w_down, final_norm):
    P = dict(meta_tokens=meta_tokens, rel_bias=rel_bias, ffn1_norm=ffn1_norm, ffn1_w_gate=ffn1_w_gate,
             ffn1_w_up=ffn1_w_up, ffn1_w_down=ffn1_w_down, mix_norm=mix_norm, w_in=w_in,
             attn_lambda_q1=attn_lambda_q1, attn_lambda_k1=attn_lambda_k1, attn_lambda_q2=attn_lambda_q2,
             attn_lambda_k2=attn_lambda_k2, attn_subln=attn_subln, w_attn_branch=w_attn_branch,
             rw_mu_prev=rw_mu_prev, rw_mu_next=rw_mu_next, rw_w0=rw_w0, rw_w2=rw_w2, rw_a0=rw_a0,
             rw_a2=rw_a2, rw_g2=rw_g2, rw_k_k=rw_k_k, rw_k_a=rw_k_a, rw_r_k=rw_r_k, rw_lnx_w=rw_lnx_w,
             rw_lnx_b=rw_lnx_b, w_rw_branch=w_rw_branch, w_out=w_out, ffn2_norm=ffn2_norm,
             ffn2_w_gate=ffn2_w_gate, ffn2_w_up=ffn2_w_up, ffn2_w_down=ffn2_w_down, final_norm=final_norm)
    y_prompt = _encode(x_prompt, P)
    y_sample = _encode(x_sample, P)
    return (y_prompt, y_sample)
```

```python
import functools
import math

import jax
import jax.numpy as jnp
from jax import lax
from jax.experimental import pallas as pl
from jax.experimental.pallas import tpu as pltpu

F32 = jnp.float32
BF16 = jnp.bfloat16

D_MODEL = 1024
N_META = 16
D_FF = 2816
EPS = 1e-6
H_A = 8
DH_A = 64
DV_A = 2 * DH_A
N_BUCKETS = 32
MAX_DISTANCE = 128
H_R = 16
N_R = 64
C_R = H_R * N_R
R_W = 64
R_A = 64
R_G = 128
LNX_EPS = 64e-5
ATT_QK = H_A * 2 * DH_A
ATT_V = H_A * DV_A
ATT_COLS = 2 * ATT_QK + ATT_V
RW_COLS = 3 * C_R + 2 * R_W + 2 * R_A + R_G
GATE_COLS = 2 * D_MODEL

LANES = 128
SUBLANES = 8
VMEM_LIMIT_BYTES = 56 * 1024 * 1024

SEQ_ALIGN = LANES
ROW_TILE = 512
PREP_ROW_TILE = 256
FF_TILE = 1408
KEY_TILE = LANES
Q_TILE = 128
SCAN_CHUNK = 64
SCAN_LANES = 256
NEG_BIG = -1e30


def _cparams(*sem):
    return pltpu.CompilerParams(dimension_semantics=sem, vmem_limit_bytes=VMEM_LIMIT_BYTES)


def _rms(x, g):
    return x * lax.rsqrt(jnp.mean(x * x, axis=-1, keepdims=True) + EPS) * g


def _bdot(a, b):
    return jnp.dot(a, b, preferred_element_type=F32)


def _dot_nt(a, b):
    return lax.dot_general(a, b, (((1,), (1,)), ((), ())), preferred_element_type=F32)


def _dot_tn(a, b):
    return lax.dot_general(a, b, (((0,), (0,)), ((), ())), preferred_element_type=F32)


def _ffn_kernel(x_ref, g_ref, wg_ref, wu_ref, wd_ref, fin_ref, o_ref, xn_ref, acc_ref, *, final_norm):
    j = pl.program_id(1)

    @pl.when(j == 0)
    def _():
        xn_ref[...] = _rms(x_ref[...], g_ref[...]).astype(BF16)
        acc_ref[...] = jnp.zeros_like(acc_ref)

    xn = xn_ref[...]
    gate = _bdot(xn, wg_ref[...])
    up = _bdot(xn, wu_ref[...])
    hid = (gate * jax.nn.sigmoid(gate) * up).astype(BF16)
    acc_ref[...] += _bdot(hid, wd_ref[...])

    @pl.when(j == pl.num_programs(1) - 1)
    def _():
        h = x_ref[...] + 0.5 * acc_ref[...]
        if final_norm:
            h = _rms(h, fin_ref[...])
        o_ref[...] = h


def _ffn(x, g, wg, wu, wd, fin, *, final_norm):
    m = x.shape[0]
    tm = min(ROW_TILE, m)
    assert m % tm == 0 and D_FF % FF_TILE == 0
    return pl.pallas_call(
        functools.partial(_ffn_kernel, final_norm=final_norm),
        grid=(m // tm, D_FF // FF_TILE),
        in_specs=[
            pl.BlockSpec((tm, D_MODEL), lambda i, j: (i, 0)),
            pl.BlockSpec((1, D_MODEL), lambda i, j: (0, 0)),
            pl.BlockSpec((D_MODEL, FF_TILE), lambda i, j: (0, j)),
            pl.BlockSpec((D_MODEL, FF_TILE), lambda i, j: (0, j)),
            pl.BlockSpec((FF_TILE, D_MODEL), lambda i, j: (j, 0)),
            pl.BlockSpec((1, D_MODEL), lambda i, j: (0, 0)),
        ],
        out_specs=pl.BlockSpec((tm, D_MODEL), lambda i, j: (i, 0)),
        out_shape=jax.ShapeDtypeStruct((m, D_MODEL), F32),
        scratch_shapes=[pltpu.VMEM((tm, D_MODEL), BF16), pltpu.VMEM((tm, D_MODEL), F32)],
        compiler_params=_cparams("parallel", "arbitrary"),
        name="ffn_final" if final_norm else "ffn",
    )(x, g, wg, wu, wd, fin)


def _proj_kernel(h_ref, g_ref, w_ref, o_ref, *, q_cols):
    u = _rms(h_ref[...], g_ref[...]).astype(BF16)
    p = _bdot(u, w_ref[...])
    if q_cols:
        o_ref[:, :q_cols] = (p[:, :q_cols] * (DH_A ** -0.5)).astype(o_ref.dtype)
        o_ref[:, q_cols:] = p[:, q_cols:].astype(o_ref.dtype)
    else:
        o_ref[...] = p.astype(o_ref.dtype)


def _proj(h, g, w, out_dtype, q_cols=0, name="proj"):
    m = h.shape[0]
    n = w.shape[1]
    tm = min(ROW_TILE, m)
    assert m % tm == 0
    return pl.pallas_call(
        functools.partial(_proj_kernel, q_cols=q_cols),
        grid=(m // tm,),
        in_specs=[
            pl.BlockSpec((tm, D_MODEL), lambda i: (i, 0)),
            pl.BlockSpec((1, D_MODEL), lambda i: (0, 0)),
            pl.BlockSpec((D_MODEL, n), lambda i: (0, 0)),
        ],
        out_specs=pl.BlockSpec((tm, n), lambda i: (i, 0)),
        out_shape=jax.ShapeDtypeStruct((m, n), out_dtype),
        compiler_params=_cparams("parallel"),
        name=name,
    )(h, g, w)


def _bias_tiles_kernel(tab_ref, o_ref, *, tq):
    j = pl.program_id(0)
    kk = lax.broadcasted_iota(jnp.int32, (KEY_TILE, tq), 0)
    qq = lax.broadcasted_iota(jnp.int32, (KEY_TILE, tq), 1)
    rel = (j - 2) * KEY_TILE + kk - qq
    nb = N_BUCKETS // 2
    max_exact = nb // 2
    n = jnp.abs(rel)
    nf = jnp.maximum(n, 1).astype(F32)
    large = max_exact + (jnp.log(nf / max_exact) / math.log(MAX_DISTANCE / max_exact) * (nb - max_exact)).astype(jnp.int32)
    large = jnp.minimum(large, nb - 1)
    bucket = (rel > 0).astype(jnp.int32) * nb + jnp.where(n < max_exact, n, large)
    for h in range(H_A):
        acc = jnp.zeros((KEY_TILE, tq), F32)
        for b in range(N_BUCKETS):
            acc = jnp.where(bucket == b, tab_ref[b, h], acc)
        o_ref[h, 0] = acc


def _bias_tiles(rel_bias, tq):
    nb = tq // KEY_TILE + 4
    return pl.pallas_call(
        functools.partial(_bias_tiles_kernel, tq=tq),
        grid=(nb,),
        in_specs=[pl.BlockSpec(memory_space=pltpu.SMEM)],
        out_specs=pl.BlockSpec((H_A, 1, KEY_TILE, tq), lambda j: (0, j, 0, 0)),
        out_shape=jax.ShapeDtypeStruct((H_A, nb, KEY_TILE, tq), F32),
        compiler_params=_cparams("arbitrary"),
        name="bias_tiles",
    )(rel_bias)


def _attn_kernel(lq1_ref, lk1_ref, lq2_ref, lk2_ref, q_ref, k_ref, v_ref, bias_ref, g_ref, o_ref,
                 s1_ref, s2_ref, vt_ref, *, n_valid, tq, lam_init):
    qi = pl.program_id(2)
    lp = k_ref.shape[1]
    nk = lp // KEY_TILE
    nb = bias_ref.shape[1]
    rows8 = KEY_TILE // SUBLANES

    @pl.when(qi == 0)
    def _():
        for t in range(nk):
            vt_ref[t] = v_ref[0, t * KEY_TILE:(t + 1) * KEY_TILE, :].astype(F32).T.astype(BF16)

    lam = (jnp.exp(jnp.sum(lq1_ref[...] * lk1_ref[...], axis=-1, keepdims=True))
           - jnp.exp(jnp.sum(lq2_ref[...] * lk2_ref[...], axis=-1, keepdims=True)) + lam_init)

    q = q_ref[0]
    lane = lax.broadcasted_iota(jnp.int32, q.shape, 1)
    zero = jnp.zeros_like(q)
    q1 = jnp.where(lane < DH_A, q, zero)
    q2 = jnp.where(lane >= DH_A, q, zero)

    def scores(kt):
        kt_rows = pl.ds(pl.multiple_of(kt * KEY_TILE, KEY_TILE), KEY_TILE)
        k_t = k_ref[0, kt_rows, :]
        idx = jnp.clip(kt - qi * (tq // KEY_TILE) + 2, 0, nb - 1)
        bt = bias_ref[0, idx]
        return _dot_nt(k_t, q1) + bt, _dot_nt(k_t, q2) + bt, kt_rows

    def fold(acc, x, op):
        for r8 in range(rows8):
            acc = op(acc, x[r8 * SUBLANES:(r8 + 1) * SUBLANES, :])
        return acc

    def phase1(kt, carry):
        m1, m2 = carry
        s1, s2, kt_rows = scores(kt)
        s1_ref[kt_rows, :] = s1
        s2_ref[kt_rows, :] = s2
        return fold(m1, s1, jnp.maximum), fold(m2, s2, jnp.maximum)

    n_full = n_valid // KEY_TILE
    neg = jnp.full((SUBLANES, tq), NEG_BIG, F32)
    m1, m2 = lax.fori_loop(0, n_full, phase1, (neg, neg))
    if n_full < nk:
        s1, s2, kt_rows = scores(n_full)
        valid = lax.broadcasted_iota(jnp.int32, s1.shape, 0) < (n_valid - n_full * KEY_TILE)
        s1 = jnp.where(valid, s1, NEG_BIG)
        s2 = jnp.where(valid, s2, NEG_BIG)
        s1_ref[kt_rows, :] = s1
        s2_ref[kt_rows, :] = s2
        m1 = fold(m1, s1, jnp.maximum)
        m2 = fold(m2, s2, jnp.maximum)
    m1 = jnp.max(m1, axis=0, keepdims=True)
    m2 = jnp.max(m2, axis=0, keepdims=True)

    def phase2(kt, carry):
        l1, l2 = carry
        kt_rows = pl.ds(pl.multiple_of(kt * KEY_TILE, KEY_TILE), KEY_TILE)
        e1 = jnp.exp(s1_ref[kt_rows, :] - m1)
        e2 = jnp.exp(s2_ref[kt_rows, :] - m2)
        s1_ref[kt_rows, :] = e1
        s2_ref[kt_rows, :] = e2
        return fold(l1, e1, jnp.add), fold(l2, e2, jnp.add)

    zero8 = jnp.zeros((SUBLANES, tq), F32)
    l1, l2 = lax.fori_loop(0, nk, phase2, (zero8, zero8))
    inv1 = 1.0 / jnp.sum(l1, axis=0, keepdims=True)
    inv2 = lam / jnp.sum(l2, axis=0, keepdims=True)

    def phase3(kt, acc):
        kt_rows = pl.ds(pl.multiple_of(kt * KEY_TILE, KEY_TILE), KEY_TILE)
        p = (s1_ref[kt_rows, :] * inv1 - s2_ref[kt_rows, :] * inv2).astype(BF16)
        return acc + _bdot(vt_ref[kt], p)

    acc = lax.fori_loop(0, nk, phase3, jnp.zeros((DV_A, tq), F32))
    ms = jnp.mean(acc * acc, axis=0, keepdims=True)
    o = acc * lax.rsqrt(ms + EPS) * g_ref[...] * (1.0 - lam_init)
    o_ref[0] = o.T.astype(o_ref.dtype)


def _attention(qkv, bias_tiles, subln_g, lq1, lk1, lq2, lk2, *, n_valid, tq, lam_init):
    b, lp, _ = qkv.shape
    nk = lp // KEY_TILE
    nb = bias_tiles.shape[1]
    k_blk = ATT_QK // LANES
    v_blk = 2 * ATT_QK // LANES
    vec = pl.BlockSpec((1, DH_A), lambda bi, h, qi: (0, 0))
    return pl.pallas_call(
        functools.partial(_attn_kernel, n_valid=n_valid, tq=tq, lam_init=lam_init),
        grid=(b, H_A, pl.cdiv(lp, tq)),
        in_specs=[
            vec, vec, vec, vec,
            pl.BlockSpec((1, tq, LANES), lambda bi, h, qi: (bi, qi, h)),
            pl.BlockSpec((1, lp, LANES), lambda bi, h, qi: (bi, 0, k_blk + h)),
            pl.BlockSpec((1, lp, LANES), lambda bi, h, qi: (bi, 0, v_blk + h)),
            pl.BlockSpec((1, nb, KEY_TILE, tq), lambda bi, h, qi: (h, 0, 0, 0)),
            pl.BlockSpec((DV_A, 1), lambda bi, h, qi: (0, 0)),
        ],
        out_specs=pl.BlockSpec((1, tq, LANES), lambda bi, h, qi: (bi, qi, h)),
        out_shape=jax.ShapeDtypeStruct((b, lp, ATT_V), BF16),
        scratch_shapes=[
            pltpu.VMEM((lp, tq), F32),
            pltpu.VMEM((lp, tq), F32),
            pltpu.VMEM((nk, DV_A, KEY_TILE), BF16),
        ],
        compiler_params=_cparams("parallel", "parallel", "arbitrary"),
        name="diff_attention",
    )(lq1, lk1, lq2, lk2, qkv, qkv, qkv, bias_tiles, subln_g)


def _segment_ones(width, scale):
    r = lax.broadcasted_iota(jnp.int32, (width, width), 0) // N_R
    c = lax.broadcasted_iota(jnp.int32, (width, width), 1) // N_R
    return jnp.where(r == c, scale, 0.0).astype(BF16)


def _segment_sum(x, seg):
    width = seg.shape[0]
    outs = []
    for c0 in range(0, x.shape[1], width):
        xs = x[:, c0:c0 + width]
        hi = xs.astype(BF16)
        lo = (xs - hi.astype(F32)).astype(BF16)
        outs.append(_bdot(hi, seg) + _bdot(lo, seg))
    return jnp.concatenate(outs, axis=1)


def _prep_kernel(p_ref, hp_ref, hn_ref, mup_ref, mun_ref, w0_ref, w2_ref, a0_ref, a2_ref, g2_ref,
                 kk_ref, ka_ref, rk_ref,
                 r_ref, v_ref, an_ref, lwf_ref, kdf_ref, bf_ref, lwb_ref, kdb_ref, bb_ref, g_ref, bonus_ref,
                 *, lp, n_valid):
    i = pl.program_id(0)
    tm = p_ref.shape[0]
    p = p_ref[...]
    row = lax.broadcasted_iota(jnp.int32, p.shape, 0)
    prev_row = jnp.where(i == 0, 0.0, hp_ref[SUBLANES - 1:SUBLANES, :])
    next_row = jnp.where(i == pl.num_programs(0) - 1, 0.0, hn_ref[0:1, :])
    prev = jnp.where(row == 0, prev_row, pltpu.roll(p, 1, 0))
    nxt = jnp.where(row == tm - 1, next_row, pltpu.roll(p, tm - 1, 0))
    x = p + mup_ref[...] * (prev - p) + mun_ref[...] * (nxt - p)

    c_wd = 3 * C_R
    c_ad = c_wd + 2 * R_W
    c_gd = c_ad + 2 * R_A
    r = x[:, 0:C_R]
    k = x[:, C_R:2 * C_R]
    v = x[:, 2 * C_R:3 * C_R]
    wd = jnp.tanh(x[:, c_wd:c_ad]).astype(BF16)
    ad = x[:, c_ad:c_gd].astype(BF16)
    gd = jax.nn.sigmoid(x[:, c_gd:c_gd + R_G]).astype(BF16)

    z = w0_ref[...] + _bdot(wd, w2_ref[...])
    softplus = jnp.maximum(-z, 0.0) + jnp.log1p(jnp.exp(-jnp.abs(z)))
    lw = -jnp.exp(-softplus - 0.5)
    a = jax.nn.sigmoid(a0_ref[...] + _bdot(ad, a2_ref[...]))
    g_ref[...] = _bdot(gd, g2_ref[...])

    seg = _segment_ones(2 * LANES, 1.0)
    kk = k * kk_ref[...]
    kk = kk / jnp.maximum(jnp.sqrt(_segment_sum(kk * kk, seg)), 1e-12)

    a_f = a[:, :C_R]
    a_b = a[:, C_R:]
    kd_f = k * (1.0 + (a_f - 1.0) * ka_ref[...])
    kd_b = k * (1.0 + (a_b - 1.0) * ka_ref[...])
    pos = (i * tm + lax.broadcasted_iota(jnp.int32, r.shape, 0)) % lp
    r_ref[...] = r
    v_ref[...] = jnp.where(pos < n_valid, v, 0.0)
    an_ref[...] = -kk
    lwf_ref[...] = lw[:, :C_R]
    lwb_ref[...] = lw[:, C_R:]
    kdf_ref[...] = kd_f
    kdb_ref[...] = kd_b
    bf_ref[...] = kk * a_f
    bb_ref[...] = kk * a_b
    bonus_ref[...] = _segment_sum(r * rk_ref[...] * (kd_f + kd_b), seg) * v


def _rwkv_prep(p_rw, mu_prev, mu_next, w0, w2, a0, a2, g2, k_k, k_a, r_k, *, lp, n_valid):
    m = p_rw.shape[0]
    tm = min(PREP_ROW_TILE, m)
    assert m % tm == 0
    h8 = tm // SUBLANES
    last8 = m // SUBLANES - 1
    full = lambda shape: pl.BlockSpec(shape, lambda i: tuple(0 for _ in shape))
    row_out = pl.BlockSpec((tm, C_R), lambda i: (i, 0))
    return pl.pallas_call(
        functools.partial(_prep_kernel, lp=lp, n_valid=n_valid),
        grid=(m // tm,),
        in_specs=[
            pl.BlockSpec((tm, RW_COLS), lambda i: (i, 0)),
            pl.BlockSpec((SUBLANES, RW_COLS), lambda i: (jnp.maximum(i * h8 - 1, 0), 0)),
            pl.BlockSpec((SUBLANES, RW_COLS), lambda i: (jnp.minimum((i + 1) * h8, last8), 0)),
            full((1, RW_COLS)), full((1, RW_COLS)),
            full((1, 2 * C_R)), full((2 * R_W, 2 * C_R)),
            full((1, 2 * C_R)), full((2 * R_A, 2 * C_R)),
            full((R_G, C_R)),
            full((1, C_R)), full((1, C_R)), full((1, C_R)),
        ],
        out_specs=[row_out] * 11,
        out_shape=[jax.ShapeDtypeStruct((m, C_R), F32)] * 11,
        compiler_params=_cparams("parallel"),
        name="rwkv_prep",
    )(p_rw, p_rw, p_rw, mu_prev, mu_next, w0, w2, a0, a2, g2, k_k, k_a, r_k)


def _scan_kernel(r_ref, lw_ref, kd_ref, v_ref, an_ref, b_ref, y_ref, s_ref, *, reverse):
    c = SCAN_CHUNK
    c2 = 2 * c
    tb = r_ref.shape[1]
    n_pairs = r_ref.shape[2] // LANES

    @pl.when(pl.program_id(2) == 0)
    def _():
        s_ref[...] = jnp.zeros_like(s_ref)

    def before(s, t, strict):
        if reverse:
            return (s > t) if strict else (s >= t)
        return (s < t) if strict else (s <= t)

    rc = lax.broadcasted_iota(jnp.int32, (c, c), 0)
    cc = lax.broadcasted_iota(jnp.int32, (c, c), 1)
    cum_mat = jnp.where(before(cc, rc, False), 1.0, 0.0).astype(BF16)
    r2 = lax.broadcasted_iota(jnp.int32, (c2, c2), 0)
    q2 = lax.broadcasted_iota(jnp.int32, (c2, c2), 1)
    same_head = (r2 >= c) == (q2 >= c)
    t2 = jnp.where(r2 >= c, r2 - c, r2)
    u2 = jnp.where(q2 >= c, q2 - c, q2)
    strict = same_head & before(u2, t2, True)
    incl = same_head & before(u2, t2, False)
    pair_mask = ((lax.broadcasted_iota(jnp.int32, (c2, LANES), 0) >= c)
                 == (lax.broadcasted_iota(jnp.int32, (c2, LANES), 1) >= N_R))
    last_row = 0 if reverse else c - 1

    def stack(x):
        return jnp.where(pair_mask, jnp.concatenate([x, x], axis=0), 0.0)

    def chunk_pair(rows, lanes, pidx):
        lw = lw_ref[0, rows, lanes]
        hi = lw.astype(BF16)
        rem = lw - hi.astype(F32)
        mid = rem.astype(BF16)
        lo = (rem - mid.astype(F32)).astype(BF16)
        cl = _bdot(cum_mat, hi) + _bdot(cum_mat, mid) + _bdot(cum_mat, lo)
        e_neg = jnp.exp(-cl)
        ab = stack(an_ref[0, rows, lanes] * jnp.exp(cl - lw)).astype(BF16)
        rb = stack(r_ref[0, rows, lanes] * jnp.exp(cl)).astype(BF16)
        bt = stack(b_ref[0, rows, lanes] * e_neg).astype(BF16)
        kt = stack(kd_ref[0, rows, lanes] * e_neg).astype(BF16)
        vs_f = stack(v_ref[0, rows, lanes])
        vs = vs_f.astype(BF16)
        w_tot = jnp.exp(cl[last_row:last_row + 1, :])

        s0 = s_ref[pidx]
        lhs = jnp.concatenate([ab, rb], axis=0)
        bk = jnp.concatenate([bt, kt], axis=0)
        gram = _dot_nt(lhs, bk)
        n_ab = jnp.where(strict, gram[:c2, :c2], 0.0)
        a_ak = jnp.where(strict, gram[:c2, c2:], 0.0)
        a_rb = jnp.where(incl, gram[c2:, :c2], 0.0)
        a_rk = jnp.where(incl, gram[c2:, c2:], 0.0)
        xs = _dot_nt(lhs, s0.astype(BF16))

        u = xs[:c2] + _bdot(a_ak.astype(BF16), vs)
        pw = n_ab
        steps = c.bit_length() - 1
        for it in range(steps):
            pw_b = pw.astype(BF16)
            u = u + _bdot(pw_b, u.astype(BF16))
            if it < steps - 1:
                pw = _bdot(pw_b, pw_b)
        ys = xs[c2:] + _bdot(a_rb.astype(BF16), u.astype(BF16)) + _bdot(a_rk.astype(BF16), vs)
        y_ref[0, rows, lanes] = ys[:c] + ys[c:]

        uv = jnp.concatenate([u, vs_f], axis=0).astype(BF16)
        s_ref[pidx] = (s0 + _dot_tn(uv, bk)) * w_tot

    n_chunks = tb // c

    def body(ci, carry):
        ch = (n_chunks - 1 - ci) if reverse else ci
        rows = pl.ds(pl.multiple_of(ch * c, c), c)
        for pidx in range(n_pairs):
            chunk_pair(rows, slice(pidx * LANES, (pidx + 1) * LANES), pidx)
        return carry

    lax.fori_loop(0, n_chunks, body, 0)


def _wkv_scan(r, lw, kd, v, an, b, *, reverse, tb):
    bsz, lp, _ = r.shape
    nt = lp // tb
    assert lp % tb == 0 and tb % SCAN_CHUNK == 0
    gw = SCAN_LANES
    tmap = (lambda bi, g, t: (bi, nt - 1 - t, g)) if reverse else (lambda bi, g, t: (bi, t, g))
    spec = pl.BlockSpec((1, tb, gw), tmap)
    return pl.pallas_call(
        functools.partial(_scan_kernel, reverse=reverse),
        grid=(bsz, C_R // gw, nt),
        in_specs=[spec] * 6,
        out_specs=spec,
        out_shape=jax.ShapeDtypeStruct((bsz, lp, C_R), F32),
        scratch_shapes=[pltpu.VMEM((gw // LANES, LANES, LANES), F32)],
        compiler_params=_cparams("parallel", "parallel", "arbitrary"),
        name="wkv_scan_bwd" if reverse else "wkv_scan_fwd",
    )(r, lw, kd, v, an, b)


def _merge_kernel(yf_ref, yb_ref, bonus_ref, g_ref, o_ref, gate_ref, h_ref, lnw_ref, lnb_ref,
                  wrw_ref, watt_ref, wout_ref, out_ref):
    seg = _segment_ones(2 * LANES, 1.0 / N_R)
    y = yf_ref[...] + yb_ref[...]
    d = y - _segment_sum(y, seg)
    var = _segment_sum(d * d, seg)
    yn = d * lax.rsqrt(var + LNX_EPS) * lnw_ref[...] + lnb_ref[...]
    y_rw = _bdot(((yn + bonus_ref[...]) * g_ref[...]).astype(BF16), wrw_ref[...])
    y_att = _bdot(o_ref[...], watt_ref[...])
    gs = jax.nn.sigmoid(gate_ref[...])
    merged = gs[:, :D_MODEL] * y_att + gs[:, D_MODEL:] * y_rw
    out_ref[...] = h_ref[...] + _bdot(merged.astype(BF16), wout_ref[...])


def _merge(yf, yb, bonus, g, o, gate, h, lnw, lnb, wrw, watt, wout):
    m = h.shape[0]
    tm = min(PREP_ROW_TILE, m)
    assert m % tm == 0
    rows = lambda n: pl.BlockSpec((tm, n), lambda i: (i, 0))
    full = lambda shape: pl.BlockSpec(shape, lambda i: tuple(0 for _ in shape))
    return pl.pallas_call(
        _merge_kernel,
        grid=(m // tm,),
        in_specs=[rows(C_R), rows(C_R), rows(C_R), rows(C_R), rows(ATT_V), rows(GATE_COLS), rows(D_MODEL),
                  full((1, C_R)), full((1, C_R)),
                  full((C_R, D_MODEL)), full((ATT_V, D_MODEL)), full((D_MODEL, D_MODEL))],
        out_specs=rows(D_MODEL),
        out_shape=jax.ShapeDtypeStruct((m, D_MODEL), F32),
        compiler_params=_cparams("parallel"),
        name="merge",
    )(yf, yb, bonus, g, o, gate, h, lnw, lnb, wrw, watt, wout)


def _pair_block_diag(w):
    z = jnp.zeros_like(w[0])
    return jnp.concatenate([jnp.concatenate([w[0], z], axis=1), jnp.concatenate([z, w[1]], axis=1)], axis=0)


def _scan_time_block(lp):
    for tb in (512, 384, 256, 128):
        if lp % tb == 0:
            return tb
    raise ValueError(lp)


def _encode(x, P, bias_tiles):
    b, seq, _ = x.shape
    n_valid = N_META + seq
    lp = (n_valid // SEQ_ALIGN + 1) * SEQ_ALIGN
    m = b * lp
    meta = jnp.broadcast_to(P["meta_tokens"].astype(x.dtype)[None], (b, N_META, D_MODEL))
    h0 = jnp.concatenate([meta, x, jnp.zeros((b, lp - n_valid, D_MODEL), x.dtype)], axis=1).reshape(m, D_MODEL)

    row = lambda a: a.reshape(1, -1)
    bf = lambda a: a.astype(BF16)
    l = 0
    h1 = _ffn(h0, row(P["ffn1_norm"][l]), bf(P["ffn1_w_gate"][l]), bf(P["ffn1_w_up"][l]), bf(P["ffn1_w_down"][l]),
              row(P["final_norm"]), final_norm=False)

    w_in = bf(P["w_in"][l])
    g_mix = row(P["mix_norm"][l])
    qkv = _proj(h1, g_mix, w_in[:, :ATT_COLS], BF16, q_cols=ATT_QK, name="proj_qkv")
    p_rw = _proj(h1, g_mix, w_in[:, ATT_COLS:ATT_COLS + RW_COLS], F32, name="proj_rw")
    p_gate = _proj(h1, g_mix, w_in[:, ATT_COLS + RW_COLS:], F32, name="proj_gate")

    lam_init = 0.8 - 0.6 * math.exp(-0.3 * l)
    o_att = _attention(qkv.reshape(b, lp, ATT_COLS), bias_tiles, P["attn_subln"][l].reshape(DV_A, 1),
                       row(P["attn_lambda_q1"][l]), row(P["attn_lambda_k1"][l]),
                       row(P["attn_lambda_q2"][l]), row(P["attn_lambda_k2"][l]),
                       n_valid=n_valid, tq=Q_TILE, lam_init=lam_init).reshape(m, ATT_V)

    (r, v, an, lw_f, kd_f, b_f, lw_b, kd_b, b_b, g, bonus) = _rwkv_prep(
        p_rw, row(P["rw_mu_prev"][l]), row(P["rw_mu_next"][l]),
        row(P["rw_w0"][l]), bf(_pair_block_diag(P["rw_w2"][l])),
        row(P["rw_a0"][l]), bf(_pair_block_diag(P["rw_a2"][l])),
        bf(P["rw_g2"][l]), row(P["rw_k_k"][l]), row(P["rw_k_a"][l]), row(P["rw_r_k"][l]),
        lp=lp, n_valid=n_valid)
    seq3 = lambda a: a.reshape(b, lp, C_R)
    tb = _scan_time_block(lp)
    y_f = _wkv_scan(seq3(r), seq3(lw_f), seq3(kd_f), seq3(v), seq3(an), seq3(b_f), reverse=False, tb=tb)
    y_b = _wkv_scan(seq3(r), seq3(lw_b), seq3(kd_b), seq3(v), seq3(an), seq3(b_b), reverse=True, tb=tb)

    h2 = _merge(y_f.reshape(m, C_R), y_b.reshape(m, C_R), bonus, g, o_att, p_gate, h1,
                row(P["rw_lnx_w"][l]), row(P["rw_lnx_b"][l]),
                bf(P["w_rw_branch"][l]), bf(P["w_attn_branch"][l]), bf(P["w_out"][l]))
    out = _ffn(h2, row(P["ffn2_norm"][l]), bf(P["ffn2_w_gate"][l]), bf(P["ffn2_w_up"][l]), bf(P["ffn2_w_down"][l]),
               row(P["final_norm"]), final_norm=True)
    return out.reshape(b, lp, D_MODEL)[:, N_META:n_valid]


def kernel(x_prompt, x_sample, meta_tokens, rel_bias, ffn1_norm, ffn1_w_gate, ffn1_w_up, ffn1_w_down,
           mix_norm, w_in, attn_lambda_q1, attn_lambda_k1, attn_lambda_q2, attn_lambda_k2, attn_subln,
           w_attn_branch, rw_mu_prev, rw_mu_next, rw_w0, rw_w2, rw_a0, rw_a2, rw_g2, rw_k_k, rw_k_a,
           rw_r_k, rw_lnx_w, rw_lnx_b, w_rw_branch, w_out, ffn2_norm, ffn2_w_gate, ffn2_w_up,
           ffn2_w_down, final_norm):
    assert ffn1_norm.shape[0] == 1, "single layer"
    P = dict(meta_tokens=meta_tokens, ffn1_norm=ffn1_norm, ffn1_w_gate=ffn1_w_gate,
             ffn1_w_up=ffn1_w_up, ffn1_w_down=ffn1_w_down, mix_norm=mix_norm, w_in=w_in,
             attn_lambda_q1=attn_lambda_q1, attn_lambda_k1=attn_lambda_k1, attn_lambda_q2=attn_lambda_q2,
             attn_lambda_k2=attn_lambda_k2, attn_subln=attn_subln, w_attn_branch=w_attn_branch,
             rw_mu_prev=rw_mu_prev, rw_mu_next=rw_mu_next, rw_w0=rw_w0, rw_w2=rw_w2, rw_a0=rw_a0,
             rw_a2=rw_a2, rw_g2=rw_g2, rw_k_k=rw_k_k, rw_k_a=rw_k_a, rw_r_k=rw_r_k, rw_lnx_w=rw_lnx_w,
             rw_lnx_b=rw_lnx_b, w_rw_branch=w_rw_branch, w_out=w_out, ffn2_norm=ffn2_norm,
             ffn2_w_gate=ffn2_w_gate, ffn2_w_up=ffn2_w_up, ffn2_w_down=ffn2_w_down, final_norm=final_norm)
    bias_tiles = _bias_tiles(rel_bias, Q_TILE)
    return (_encode(x_prompt, P, bias_tiles), _encode(x_sample, P, bias_tiles))
```

```python
import functools
import math

import jax
import jax.numpy as jnp
from jax import lax
from jax.experimental import pallas as pl
from jax.experimental.pallas import tpu as pltpu

F32 = jnp.float32
BF16 = jnp.bfloat16

D_MODEL = 1024
N_META = 16
D_FF = 2816
EPS = 1e-6
H_A = 8
DH_A = 64
DV_A = 2 * DH_A
N_BUCKETS = 32
MAX_DISTANCE = 128
H_R = 16
N_R = 64
C_R = H_R * N_R
R_W = 64
R_A = 64
R_G = 128
LNX_EPS = 64e-5
ATT_QK = H_A * 2 * DH_A
ATT_V = H_A * DV_A
ATT_COLS = 2 * ATT_QK + ATT_V
RW_COLS = 3 * C_R + 2 * R_W + 2 * R_A + R_G
GATE_COLS = 2 * D_MODEL

LANES = 128
SUBLANES = 8
VMEM_LIMIT_BYTES = 56 * 1024 * 1024

SEQ_ALIGN = LANES
ROW_TILE = 512
PREP_ROW_TILE = 256
FF_TILE = 1408
KEY_TILE = LANES
Q_TILE = 256
SCAN_CHUNK = 64
NEG_BIG = -1e30


def _cparams(*sem):
    return pltpu.CompilerParams(dimension_semantics=sem, vmem_limit_bytes=VMEM_LIMIT_BYTES)


def _rms(x, g):
    return x * lax.rsqrt(jnp.mean(x * x, axis=-1, keepdims=True) + EPS) * g


def _bdot(a, b):
    return jnp.dot(a, b, preferred_element_type=F32)


def _dot_nt(a, b):
    return lax.dot_general(a, b, (((1,), (1,)), ((), ())), preferred_element_type=F32)


def _dot_tn(a, b):
    return lax.dot_general(a, b, (((0,), (0,)), ((), ())), preferred_element_type=F32)


def _ffn_kernel(x_ref, g_ref, wg_ref, wu_ref, wd_ref, fin_ref, o_ref, xn_ref, acc_ref, *, final_norm):
    j = pl.program_id(1)

    @pl.when(j == 0)
    def _():
        xn_ref[...] = _rms(x_ref[...], g_ref[...]).astype(BF16)
        acc_ref[...] = jnp.zeros_like(acc_ref)

    xn = xn_ref[...]
    gate = _bdot(xn, wg_ref[...])
    up = _bdot(xn, wu_ref[...])
    hid = (gate * jax.nn.sigmoid(gate) * up).astype(BF16)
    acc_ref[...] += _bdot(hid, wd_ref[...])

    @pl.when(j == pl.num_programs(1) - 1)
    def _():
        h = x_ref[...] + 0.5 * acc_ref[...]
        if final_norm:
            h = _rms(h, fin_ref[...])
        o_ref[...] = h


def _ffn(x, g, wg, wu, wd, fin, *, final_norm):
    m = x.shape[0]
    tm = min(ROW_TILE, m)
    assert m % tm == 0 and D_FF % FF_TILE == 0
    return pl.pallas_call(
        functools.partial(_ffn_kernel, final_norm=final_norm),
        grid=(m // tm, D_FF // FF_TILE),
        in_specs=[
            pl.BlockSpec((tm, D_MODEL), lambda i, j: (i, 0)),
            pl.BlockSpec((1, D_MODEL), lambda i, j: (0, 0)),
            pl.BlockSpec((D_MODEL, FF_TILE), lambda i, j: (0, j)),
            pl.BlockSpec((D_MODEL, FF_TILE), lambda i, j: (0, j)),
            pl.BlockSpec((FF_TILE, D_MODEL), lambda i, j: (j, 0)),
            pl.BlockSpec((1, D_MODEL), lambda i, j: (0, 0)),
        ],
        out_specs=pl.BlockSpec((tm, D_MODEL), lambda i, j: (i, 0)),
        out_shape=jax.ShapeDtypeStruct((m, D_MODEL), F32),
        scratch_shapes=[pltpu.VMEM((tm, D_MODEL), BF16), pltpu.VMEM((tm, D_MODEL), F32)],
        compiler_params=_cparams("parallel", "arbitrary"),
        name="ffn_final" if final_norm else "ffn",
    )(x, g, wg, wu, wd, fin)


def _proj_kernel(h_ref, g_ref, w_ref, o_ref, *, q_cols):
    u = _rms(h_ref[...], g_ref[...]).astype(BF16)
    p = _bdot(u, w_ref[...])
    if q_cols:
        o_ref[:, :q_cols] = (p[:, :q_cols] * (DH_A ** -0.5)).astype(o_ref.dtype)
        o_ref[:, q_cols:] = p[:, q_cols:].astype(o_ref.dtype)
    else:
        o_ref[...] = p.astype(o_ref.dtype)


def _proj(h, g, w, out_dtype, q_cols=0, name="proj"):
    m = h.shape[0]
    n = w.shape[1]
    tm = min(ROW_TILE, m)
    assert m % tm == 0
    return pl.pallas_call(
        functools.partial(_proj_kernel, q_cols=q_cols),
        grid=(m // tm,),
        in_specs=[
            pl.BlockSpec((tm, D_MODEL), lambda i: (i, 0)),
            pl.BlockSpec((1, D_MODEL), lambda i: (0, 0)),
            pl.BlockSpec((D_MODEL, n), lambda i: (0, 0)),
        ],
        out_specs=pl.BlockSpec((tm, n), lambda i: (i, 0)),
        out_shape=jax.ShapeDtypeStruct((m, n), out_dtype),
        compiler_params=_cparams("parallel"),
        name=name,
    )(h, g, w)


def _bias_tiles_kernel(tab_ref, o_ref, *, tq):
    j = pl.program_id(0)
    kk = lax.broadcasted_iota(jnp.int32, (KEY_TILE, tq), 0)
    qq = lax.broadcasted_iota(jnp.int32, (KEY_TILE, tq), 1)
    rel = (j - 2) * KEY_TILE + kk - qq
    nb = N_BUCKETS // 2
    max_exact = nb // 2
    n = jnp.abs(rel)
    nf = jnp.maximum(n, 1).astype(F32)
    large = max_exact + (jnp.log(nf / max_exact) / math.log(MAX_DISTANCE / max_exact) * (nb - max_exact)).astype(jnp.int32)
    large = jnp.minimum(large, nb - 1)
    bucket = (rel > 0).astype(jnp.int32) * nb + jnp.where(n < max_exact, n, large)
    for h in range(H_A):
        acc = jnp.zeros((KEY_TILE, tq), F32)
        for b in range(N_BUCKETS):
            acc = jnp.where(bucket == b, tab_ref[b, h], acc)
        o_ref[h, 0] = acc


def _bias_tiles(rel_bias, tq):
    nb = tq // KEY_TILE + 4
    return pl.pallas_call(
        functools.partial(_bias_tiles_kernel, tq=tq),
        grid=(nb,),
        in_specs=[pl.BlockSpec(memory_space=pltpu.SMEM)],
        out_specs=pl.BlockSpec((H_A, 1, KEY_TILE, tq), lambda j: (0, j, 0, 0)),
        out_shape=jax.ShapeDtypeStruct((H_A, nb, KEY_TILE, tq), F32),
        compiler_params=_cparams("arbitrary"),
        name="bias_tiles",
    )(rel_bias)


def _attn_kernel(lq1_ref, lk1_ref, lq2_ref, lk2_ref, q_ref, k_ref, v_ref, bias_ref, g_ref, o_ref,
                 s1_ref, s2_ref, p_ref, vt_ref, *, n_valid, tq, lam_init):
    qi = pl.program_id(2)
    lp = k_ref.shape[1]
    nk = lp // KEY_TILE
    nb = bias_ref.shape[1]
    rows8 = KEY_TILE // SUBLANES

    @pl.when(qi == 0)
    def _():
        for t in range(nk):
            cols = slice(t * KEY_TILE, (t + 1) * KEY_TILE)
            vt_ref[:, cols] = v_ref[0, cols, :].astype(F32).T.astype(BF16)

    lam = (jnp.exp(jnp.sum(lq1_ref[...] * lk1_ref[...], axis=-1, keepdims=True))
           - jnp.exp(jnp.sum(lq2_ref[...] * lk2_ref[...], axis=-1, keepdims=True)) + lam_init)

    q = q_ref[0]
    lane = lax.broadcasted_iota(jnp.int32, q.shape, 1)
    qrow = lax.broadcasted_iota(jnp.int32, q.shape, 0)
    zero = jnp.zeros_like(q)
    in_range = qrow < lp - qi * tq
    q1 = jnp.where(in_range & (lane < DH_A), q, zero)
    q2 = jnp.where(in_range & (lane >= DH_A), q, zero)
    s1_ref[...] = _dot_nt(k_ref[0], q1)
    s2_ref[...] = _dot_nt(k_ref[0], q2)

    def tile_rows(kt):
        return pl.ds(pl.multiple_of(kt * KEY_TILE, KEY_TILE), KEY_TILE)

    def fold(acc, x, op):
        for r8 in range(rows8):
            acc = op(acc, x[r8 * SUBLANES:(r8 + 1) * SUBLANES, :])
        return acc

    def add_bias(kt, carry, n_keys=KEY_TILE):
        m1, m2 = carry
        rows = tile_rows(kt)
        bt = bias_ref[0, jnp.clip(kt - qi * (tq // KEY_TILE) + 2, 0, nb - 1)]
        s1 = s1_ref[rows, :] + bt
        s2 = s2_ref[rows, :] + bt
        if n_keys < KEY_TILE:
            valid = lax.broadcasted_iota(jnp.int32, s1.shape, 0) < n_keys
            s1 = jnp.where(valid, s1, NEG_BIG)
            s2 = jnp.where(valid, s2, NEG_BIG)
        s1_ref[rows, :] = s1
        s2_ref[rows, :] = s2
        return fold(m1, s1, jnp.maximum), fold(m2, s2, jnp.maximum)

    n_full = n_valid // KEY_TILE
    neg = jnp.full((SUBLANES, tq), NEG_BIG, F32)
    m1, m2 = lax.fori_loop(0, n_full, add_bias, (neg, neg), unroll=2)
    if n_full < nk:
        m1, m2 = add_bias(n_full, (m1, m2), n_keys=n_valid - n_full * KEY_TILE)
    m1 = jnp.max(m1, axis=0, keepdims=True)
    m2 = jnp.max(m2, axis=0, keepdims=True)

    def exponentiate(kt, carry):
        l1, l2 = carry
        rows = tile_rows(kt)
        e1 = jnp.exp(s1_ref[rows, :] - m1)
        e2 = jnp.exp(s2_ref[rows, :] - m2)
        s1_ref[rows, :] = e1
        s2_ref[rows, :] = e2
        return fold(l1, e1, jnp.add), fold(l2, e2, jnp.add)

    zero8 = jnp.zeros((SUBLANES, tq), F32)
    l1, l2 = lax.fori_loop(0, nk, exponentiate, (zero8, zero8), unroll=2)
    inv1 = 1.0 / jnp.sum(l1, axis=0, keepdims=True)
    inv2 = lam / jnp.sum(l2, axis=0, keepdims=True)

    def combine(kt, carry):
        rows = tile_rows(kt)
        p_ref[rows, :] = (s1_ref[rows, :] * inv1 - s2_ref[rows, :] * inv2).astype(BF16)
        return carry

    lax.fori_loop(0, nk, combine, 0, unroll=2)
    acc = _bdot(vt_ref[...], p_ref[...])
    ms = jnp.mean(acc * acc, axis=0, keepdims=True)
    o = acc * lax.rsqrt(ms + EPS) * g_ref[...] * (1.0 - lam_init)
    o_ref[0] = o.T.astype(o_ref.dtype)


def _attention(qkv, bias_tiles, subln_g, lq1, lk1, lq2, lk2, *, n_valid, tq, lam_init):
    b, lp, _ = qkv.shape
    nk = lp // KEY_TILE
    nb = bias_tiles.shape[1]
    k_blk = ATT_QK // LANES
    v_blk = 2 * ATT_QK // LANES
    vec = pl.BlockSpec((1, DH_A), lambda bi, h, qi: (0, 0))
    return pl.pallas_call(
        functools.partial(_attn_kernel, n_valid=n_valid, tq=tq, lam_init=lam_init),
        grid=(b, H_A, pl.cdiv(lp, tq)),
        in_specs=[
            vec, vec, vec, vec,
            pl.BlockSpec((1, tq, LANES), lambda bi, h, qi: (bi, qi, h)),
            pl.BlockSpec((1, lp, LANES), lambda bi, h, qi: (bi, 0, k_blk + h)),
            pl.BlockSpec((1, lp, LANES), lambda bi, h, qi: (bi, 0, v_blk + h)),
            pl.BlockSpec((1, nb, KEY_TILE, tq), lambda bi, h, qi: (h, 0, 0, 0)),
            pl.BlockSpec((DV_A, 1), lambda bi, h, qi: (0, 0)),
        ],
        out_specs=pl.BlockSpec((1, tq, LANES), lambda bi, h, qi: (bi, qi, h)),
        out_shape=jax.ShapeDtypeStruct((b, lp, ATT_V), BF16),
        scratch_shapes=[
            pltpu.VMEM((lp, tq), F32),
            pltpu.VMEM((lp, tq), F32),
            pltpu.VMEM((lp, tq), BF16),
            pltpu.VMEM((DV_A, lp), BF16),
        ],
        compiler_params=_cparams("parallel", "parallel", "arbitrary"),
        name="diff_attention",
    )(lq1, lk1, lq2, lk2, qkv, qkv, qkv, bias_tiles, subln_g)


def _segment_ones(width, scale):
    r = lax.broadcasted_iota(jnp.int32, (width, width), 0) // N_R
    c = lax.broadcasted_iota(jnp.int32, (width, width), 1) // N_R
    return jnp.where(r == c, scale, 0.0).astype(BF16)


def _segment_sum(x, seg):
    width = seg.shape[0]
    outs = []
    for c0 in range(0, x.shape[1], width):
        xs = x[:, c0:c0 + width]
        hi = xs.astype(BF16)
        lo = (xs - hi.astype(F32)).astype(BF16)
        outs.append(_bdot(hi, seg) + _bdot(lo, seg))
    return jnp.concatenate(outs, axis=1)


def _prep_kernel(p_ref, hp_ref, hn_ref, mup_ref, mun_ref, w0_ref, w2_ref, a0_ref, a2_ref, g2_ref,
                 kk_ref, ka_ref, rk_ref,
                 r_ref, v_ref, an_ref, lwf_ref, kdf_ref, bf_ref, lwb_ref, kdb_ref, bb_ref, g_ref, bonus_ref,
                 *, lp, n_valid):
    i = pl.program_id(0)
    tm = p_ref.shape[0]
    p = p_ref[...]
    row = lax.broadcasted_iota(jnp.int32, p.shape, 0)
    prev_row = jnp.where(i == 0, 0.0, hp_ref[SUBLANES - 1:SUBLANES, :])
    next_row = jnp.where(i == pl.num_programs(0) - 1, 0.0, hn_ref[0:1, :])
    prev = jnp.where(row == 0, prev_row, pltpu.roll(p, 1, 0))
    nxt = jnp.where(row == tm - 1, next_row, pltpu.roll(p, tm - 1, 0))
    x = p + mup_ref[...] * (prev - p) + mun_ref[...] * (nxt - p)

    c_wd = 3 * C_R
    c_ad = c_wd + 2 * R_W
    c_gd = c_ad + 2 * R_A
    r = x[:, 0:C_R]
    k = x[:, C_R:2 * C_R]
    v = x[:, 2 * C_R:3 * C_R]
    wd = jnp.tanh(x[:, c_wd:c_ad]).astype(BF16)
    ad = x[:, c_ad:c_gd].astype(BF16)
    gd = jax.nn.sigmoid(x[:, c_gd:c_gd + R_G]).astype(BF16)

    z = w0_ref[...] + _bdot(wd, w2_ref[...])
    softplus = jnp.maximum(-z, 0.0) + jnp.log1p(jnp.exp(-jnp.abs(z)))
    lw = -jnp.exp(-softplus - 0.5)
    a = jax.nn.sigmoid(a0_ref[...] + _bdot(ad, a2_ref[...]))
    g_ref[...] = _bdot(gd, g2_ref[...])

    seg = _segment_ones(2 * LANES, 1.0)
    kk = k * kk_ref[...]
    kk = kk / jnp.maximum(jnp.sqrt(_segment_sum(kk * kk, seg)), 1e-12)

    a_f = a[:, :C_R]
    a_b = a[:, C_R:]
    kd_f = k * (1.0 + (a_f - 1.0) * ka_ref[...])
    kd_b = k * (1.0 + (a_b - 1.0) * ka_ref[...])
    pos = (i * tm + lax.broadcasted_iota(jnp.int32, r.shape, 0)) % lp
    r_ref[...] = r
    v_ref[...] = jnp.where(pos < n_valid, v, 0.0)
    an_ref[...] = -kk
    lwf_ref[...] = lw[:, :C_R]
    lwb_ref[...] = lw[:, C_R:]
    kdf_ref[...] = kd_f
    kdb_ref[...] = kd_b
    bf_ref[...] = kk * a_f
    bb_ref[...] = kk * a_b
    bonus_ref[...] = _segment_sum(r * rk_ref[...] * (kd_f + kd_b), seg) * v


def _rwkv_prep(p_rw, mu_prev, mu_next, w0, w2, a0, a2, g2, k_k, k_a, r_k, *, lp, n_valid):
    m = p_rw.shape[0]
    tm = min(PREP_ROW_TILE, m)
    assert m % tm == 0
    h8 = tm // SUBLANES
    last8 = m // SUBLANES - 1
    full = lambda shape: pl.BlockSpec(shape, lambda i: tuple(0 for _ in shape))
    row_out = pl.BlockSpec((tm, C_R), lambda i: (i, 0))
    return pl.pallas_call(
        functools.partial(_prep_kernel, lp=lp, n_valid=n_valid),
        grid=(m // tm,),
        in_specs=[
            pl.BlockSpec((tm, RW_COLS), lambda i: (i, 0)),
            pl.BlockSpec((SUBLANES, RW_COLS), lambda i: (jnp.maximum(i * h8 - 1, 0), 0)),
            pl.BlockSpec((SUBLANES, RW_COLS), lambda i: (jnp.minimum((i + 1) * h8, last8), 0)),
            full((1, RW_COLS)), full((1, RW_COLS)),
            full((1, 2 * C_R)), full((2 * R_W, 2 * C_R)),
            full((1, 2 * C_R)), full((2 * R_A, 2 * C_R)),
            full((R_G, C_R)),
            full((1, C_R)), full((1, C_R)), full((1, C_R)),
        ],
        out_specs=[row_out] * 11,
        out_shape=[jax.ShapeDtypeStruct((m, C_R), F32)] * 11,
        compiler_params=_cparams("parallel"),
        name="rwkv_prep",
    )(p_rw, p_rw, p_rw, mu_prev, mu_next, w0, w2, a0, a2, g2, k_k, k_a, r_k)


def _scan_kernel(r_ref, lw_ref, kd_ref, v_ref, an_ref, b_ref, y_ref,
                 s_ref, lhs_ref, tav_ref, arb_ref, arkv_ref, bt_ref, vtk_ref, wtot_ref, *, reverse):
    c = SCAN_CHUNK
    c2 = 2 * c
    tb = r_ref.shape[1]
    n_pairs = r_ref.shape[2] // LANES

    @pl.when(pl.program_id(1) == 0)
    def _():
        s_ref[...] = jnp.zeros_like(s_ref)

    def before(s, t, strict):
        if reverse:
            return (s > t) if strict else (s >= t)
        return (s < t) if strict else (s <= t)

    rc = lax.broadcasted_iota(jnp.int32, (c, c), 0)
    cc = lax.broadcasted_iota(jnp.int32, (c, c), 1)
    cum_mat = jnp.where(before(cc, rc, False), 1.0, 0.0).astype(BF16)
    r2 = lax.broadcasted_iota(jnp.int32, (c2, c2), 0)
    q2 = lax.broadcasted_iota(jnp.int32, (c2, c2), 1)
    same_head = (r2 >= c) == (q2 >= c)
    t2 = jnp.where(r2 >= c, r2 - c, r2)
    u2 = jnp.where(q2 >= c, q2 - c, q2)
    strict = same_head & before(u2, t2, True)
    incl = same_head & before(u2, t2, False)
    pair_mask = ((lax.broadcasted_iota(jnp.int32, (c2, LANES), 0) >= c)
                 == (lax.broadcasted_iota(jnp.int32, (c2, LANES), 1) >= N_R))
    last_row = 0 if reverse else c - 1

    def stack(x):
        return jnp.where(pair_mask, jnp.concatenate([x, x], axis=0), 0.0)

    eye = jnp.where(r2 == q2, 1.0, 0.0)
    steps = c.bit_length() - 1

    pairs = range(n_pairs)
    lane_slices = [slice(p * LANES, (p + 1) * LANES) for p in pairs]

    def prepare_chunk(ci, carry):
        rows = pl.ds(pl.multiple_of(ci * c, c), c)
        lw = lw_ref[0, rows, :]
        hi = lw.astype(BF16)
        rem = lw - hi.astype(F32)
        mid = rem.astype(BF16)
        lo = (rem - mid.astype(F32)).astype(BF16)
        cl = _bdot(cum_mat, hi) + _bdot(cum_mat, mid) + _bdot(cum_mat, lo)
        e_neg = jnp.exp(-cl)
        ab_all = an_ref[0, rows, :] * jnp.exp(cl - lw)
        rb_all = r_ref[0, rows, :] * jnp.exp(cl)
        bt_all = b_ref[0, rows, :] * e_neg
        kt_all = kd_ref[0, rows, :] * e_neg
        v_all = v_ref[0, rows, :]
        w_tot = jnp.exp(cl[last_row:last_row + 1, :])

        stacked = lambda x: [stack(x[:, ls]).astype(BF16) for ls in lane_slices]
        ab, rb, bt, kt, vs = stacked(ab_all), stacked(rb_all), stacked(bt_all), stacked(kt_all), stacked(v_all)
        gram = [_dot_nt(jnp.concatenate([ab[p], rb[p]], axis=0), jnp.concatenate([bt[p], kt[p]], axis=0))
                for p in pairs]
        n_ab = [jnp.where(strict, g[:c2, :c2], 0.0) for g in gram]
        a_ak = [jnp.where(strict, g[:c2, c2:], 0.0) for g in gram]
        a_rb = [jnp.where(incl, g[c2:, :c2], 0.0).astype(BF16) for g in gram]
        a_rk = [jnp.where(incl, g[c2:, c2:], 0.0) for g in gram]
        akv = [_bdot(jnp.concatenate([a_ak[p], a_rk[p]], axis=0).astype(BF16), vs[p]) for p in pairs]
        vtk = [_dot_tn(vs[p], kt[p]) for p in pairs]

        t_mat = [eye + n for n in n_ab]
        pw = [n.astype(BF16) for n in n_ab]
        for _ in range(1, steps):
            pw = [_bdot(x, x).astype(BF16) for x in pw]
            t_mat = [t_mat[p] + _bdot(t_mat[p].astype(BF16), pw[p]) for p in pairs]
        applied = [_bdot(t_mat[p].astype(BF16), jnp.concatenate([ab[p], akv[p][:c2].astype(BF16)], axis=1))
                   for p in pairs]
        for p in pairs:
            lhs_ref[ci, p] = jnp.concatenate([applied[p][:, :LANES].astype(BF16), rb[p]], axis=0)
            tav_ref[ci, p] = applied[p][:, LANES:]
            arb_ref[ci, p] = a_rb[p]
            arkv_ref[ci, p] = akv[p][c2:]
            bt_ref[ci, p] = bt[p]
            vtk_ref[ci, p] = vtk[p]
            wtot_ref[ci, p] = jnp.broadcast_to(w_tot[:, lane_slices[p]], (SUBLANES, LANES))
        return carry

    n_chunks = tb // c

    def advance_chunk(i, carry):
        ci = (n_chunks - 1 - i) if reverse else i
        rows = pl.ds(pl.multiple_of(ci * c, c), c)
        s0 = [s_ref[p] for p in pairs]
        xs = [_dot_nt(lhs_ref[ci, p], s0[p].astype(BF16)) for p in pairs]
        u = [(xs[p][:c2] + tav_ref[ci, p]).astype(BF16) for p in pairs]
        du = [_dot_tn(u[p], bt_ref[ci, p]) for p in pairs]
        yu = [_bdot(arb_ref[ci, p], u[p]) for p in pairs]
        for p in pairs:
            s_ref[p] = (s0[p] + du[p] + vtk_ref[ci, p]) * wtot_ref[ci, p][0:1, :]
            ys = xs[p][c2:] + yu[p] + arkv_ref[ci, p]
            y_ref[0, rows, lane_slices[p]] = ys[:c] + ys[c:]
        return carry

    lax.fori_loop(0, n_chunks, prepare_chunk, 0)
    lax.fori_loop(0, n_chunks, advance_chunk, 0)


def _wkv_scan(r, lw, kd, v, an, b, *, reverse, tb):
    bsz, lp, _ = r.shape
    nt = lp // tb
    assert lp % tb == 0 and tb % SCAN_CHUNK == 0
    n_chunks = tb // SCAN_CHUNK
    n_pairs = C_R // LANES
    c2 = 2 * SCAN_CHUNK
    tmap = (lambda bi, t: (bi, nt - 1 - t, 0)) if reverse else (lambda bi, t: (bi, t, 0))
    spec = pl.BlockSpec((1, tb, C_R), tmap)
    per_chunk = lambda rows, dtype: pltpu.VMEM((n_chunks, n_pairs, rows, LANES), dtype)
    return pl.pallas_call(
        functools.partial(_scan_kernel, reverse=reverse),
        grid=(bsz, nt),
        in_specs=[spec] * 6,
        out_specs=spec,
        out_shape=jax.ShapeDtypeStruct((bsz, lp, C_R), F32),
        scratch_shapes=[
            pltpu.VMEM((n_pairs, LANES, LANES), F32),
            per_chunk(2 * c2, BF16),
            per_chunk(c2, F32),
            per_chunk(c2, BF16),
            per_chunk(c2, F32),
            per_chunk(c2, BF16),
            per_chunk(LANES, F32),
            per_chunk(SUBLANES, F32),
        ],
        compiler_params=_cparams("parallel", "arbitrary"),
        name="wkv_scan_bwd" if reverse else "wkv_scan_fwd",
    )(r, lw, kd, v, an, b)


def _merge_kernel(yf_ref, yb_ref, bonus_ref, g_ref, o_ref, gate_ref, h_ref, lnw_ref, lnb_ref,
                  wrw_ref, watt_ref, wout_ref, out_ref):
    seg = _segment_ones(2 * LANES, 1.0 / N_R)
    y = yf_ref[...] + yb_ref[...]
    d = y - _segment_sum(y, seg)
    var = _segment_sum(d * d, seg)
    yn = d * lax.rsqrt(var + LNX_EPS) * lnw_ref[...] + lnb_ref[...]
    y_rw = _bdot(((yn + bonus_ref[...]) * g_ref[...]).astype(BF16), wrw_ref[...])
    y_att = _bdot(o_ref[...], watt_ref[...])
    gs = jax.nn.sigmoid(gate_ref[...])
    merged = gs[:, :D_MODEL] * y_att + gs[:, D_MODEL:] * y_rw
    out_ref[...] = h_ref[...] + _bdot(merged.astype(BF16), wout_ref[...])


def _merge(yf, yb, bonus, g, o, gate, h, lnw, lnb, wrw, watt, wout):
    m = h.shape[0]
    tm = min(PREP_ROW_TILE, m)
    assert m % tm == 0
    rows = lambda n: pl.BlockSpec((tm, n), lambda i: (i, 0))
    full = lambda shape: pl.BlockSpec(shape, lambda i: tuple(0 for _ in shape))
    return pl.pallas_call(
        _merge_kernel,
        grid=(m // tm,),
        in_specs=[rows(C_R), rows(C_R), rows(C_R), rows(C_R), rows(ATT_V), rows(GATE_COLS), rows(D_MODEL),
                  full((1, C_R)), full((1, C_R)),
                  full((C_R, D_MODEL)), full((ATT_V, D_MODEL)), full((D_MODEL, D_MODEL))],
        out_specs=rows(D_MODEL),
        out_shape=jax.ShapeDtypeStruct((m, D_MODEL), F32),
        compiler_params=_cparams("parallel"),
        name="merge",
    )(yf, yb, bonus, g, o, gate, h, lnw, lnb, wrw, watt, wout)


def _pair_block_diag(w):
    z = jnp.zeros_like(w[0])
    return jnp.concatenate([jnp.concatenate([w[0], z], axis=1), jnp.concatenate([z, w[1]], axis=1)], axis=0)


def _scan_time_block(lp):
    for tb in (512, 384, 256, 128):
        if lp % tb == 0:
            return tb
    raise ValueError(lp)


def _encode(x, P, bias_tiles):
    b, seq, _ = x.shape
    n_valid = N_META + seq
    lp = (n_valid // SEQ_ALIGN + 1) * SEQ_ALIGN
    m = b * lp
    meta = jnp.broadcast_to(P["meta_tokens"].astype(x.dtype)[None], (b, N_META, D_MODEL))
    h0 = jnp.concatenate([meta, x, jnp.zeros((b, lp - n_valid, D_MODEL), x.dtype)], axis=1).reshape(m, D_MODEL)

    row = lambda a: a.reshape(1, -1)
    bf = lambda a: a.astype(BF16)
    l = 0
    h1 = _ffn(h0, row(P["ffn1_norm"][l]), bf(P["ffn1_w_gate"][l]), bf(P["ffn1_w_up"][l]), bf(P["ffn1_w_down"][l]),
              row(P["final_norm"]), final_norm=False)

    w_in = bf(P["w_in"][l])
    g_mix = row(P["mix_norm"][l])
    qkv = _proj(h1, g_mix, w_in[:, :ATT_COLS], BF16, q_cols=ATT_QK, name="proj_qkv")
    p_rw = _proj(h1, g_mix, w_in[:, ATT_COLS:ATT_COLS + RW_COLS], F32, name="proj_rw")
    p_gate = _proj(h1, g_mix, w_in[:, ATT_COLS + RW_COLS:], F32, name="proj_gate")

    lam_init = 0.8 - 0.6 * math.exp(-0.3 * l)
    o_att = _attention(qkv.reshape(b, lp, ATT_COLS), bias_tiles, P["attn_subln"][l].reshape(DV_A, 1),
                       row(P["attn_lambda_q1"][l]), row(P["attn_lambda_k1"][l]),
                       row(P["attn_lambda_q2"][l]), row(P["attn_lambda_k2"][l]),
                       n_valid=n_valid, tq=Q_TILE, lam_init=lam_init).reshape(m, ATT_V)

    (r, v, an, lw_f, kd_f, b_f, lw_b, kd_b, b_b, g, bonus) = _rwkv_prep(
        p_rw, row(P["rw_mu_prev"][l]), row(P["rw_mu_next"][l]),
        row(P["rw_w0"][l]), bf(_pair_block_diag(P["rw_w2"][l])),
        row(P["rw_a0"][l]), bf(_pair_block_diag(P["rw_a2"][l])),
        bf(P["rw_g2"][l]), row(P["rw_k_k"][l]), row(P["rw_k_a"][l]), row(P["rw_r_k"][l]),
        lp=lp, n_valid=n_valid)
    seq3 = lambda a: a.reshape(b, lp, C_R)
    tb = _scan_time_block(lp)
    y_f = _wkv_scan(seq3(r), seq3(lw_f), seq3(kd_f), seq3(v), seq3(an), seq3(b_f), reverse=False, tb=tb)
    y_b = _wkv_scan(seq3(r), seq3(lw_b), seq3(kd_b), seq3(v), seq3(an), seq3(b_b), reverse=True, tb=tb)

    h2 = _merge(y_f.reshape(m, C_R), y_b.reshape(m, C_R), bonus, g, o_att, p_gate, h1,
                row(P["rw_lnx_w"][l]), row(P["rw_lnx_b"][l]),
                bf(P["w_rw_branch"][l]), bf(P["w_attn_branch"][l]), bf(P["w_out"][l]))
    out = _ffn(h2, row(P["ffn2_norm"][l]), bf(P["ffn2_w_gate"][l]), bf(P["ffn2_w_up"][l]), bf(P["ffn2_w_down"][l]),
               row(P["final_norm"]), final_norm=True)
    return out.reshape(b, lp, D_MODEL)[:, N_META:n_valid]


def kernel(x_prompt, x_sample, meta_tokens, rel_bias, ffn1_norm, ffn1_w_gate, ffn1_w_up, ffn1_w_down,
           mix_norm, w_in, attn_lambda_q1, attn_lambda_k1, attn_lambda_q2, attn_lambda_k2, attn_subln,
           w_attn_branch, rw_mu_prev, rw_mu_next, rw_w0, rw_w2, rw_a0, rw_a2, rw_g2, rw_k_k, rw_k_a,
           rw_r_k, rw_lnx_w, rw_lnx_b, w_rw_branch, w_out, ffn2_norm, ffn2_w_gate, ffn2_w_up,
           ffn2_w_down, final_norm):
    assert ffn1_norm.shape[0] == 1, "single layer"
    P = dict(meta_tokens=meta_tokens, ffn1_norm=ffn1_norm, ffn1_w_gate=ffn1_w_gate,
             ffn1_w_up=ffn1_w_up, ffn1_w_down=ffn1_w_down, mix_norm=mix_norm, w_in=w_in,
             attn_lambda_q1=attn_lambda_q1, attn_lambda_k1=attn_lambda_k1, attn_lambda_q2=attn_lambda_q2,
             attn_lambda_k2=attn_lambda_k2, attn_subln=attn_subln, w_attn_branch=w_attn_branch,
             rw_mu_prev=rw_mu_prev, rw_mu_next=rw_mu_next, rw_w0=rw_w0, rw_w2=rw_w2, rw_a0=rw_a0,
             rw_a2=rw_a2, rw_g2=rw_g2, rw_k_k=rw_k_k, rw_k_a=rw_k_a, rw_r_k=rw_r_k, rw_lnx_w=rw_lnx_w,
             rw_lnx_b=rw_lnx_b, w_rw_branch=w_rw_branch, w_out=w_out, ffn2_norm=ffn2_norm,
             ffn2_w_gate=ffn2_w_gate, ffn2_w_up=ffn2_w_up, ffn2_w_down=ffn2_w_down, final_norm=final_norm)
    bias_tiles = _bias_tiles(rel_bias, Q_TILE)
    return (_encode(x_prompt, P, bias_tiles), _encode(x_sample, P, bias_tiles))
```

```python
import functools
import math

import jax
import jax.numpy as jnp
from jax import lax
from jax.experimental import pallas as pl
from jax.experimental.pallas import tpu as pltpu

F32 = jnp.float32
BF16 = jnp.bfloat16

D_MODEL = 1024
N_META = 16
D_FF = 2816
EPS = 1e-6
H_A = 8
DH_A = 64
DV_A = 2 * DH_A
N_BUCKETS = 32
MAX_DISTANCE = 128
H_R = 16
N_R = 64
C_R = H_R * N_R
R_W = 64
R_A = 64
R_G = 128
LNX_EPS = 64e-5
ATT_QK = H_A * 2 * DH_A
ATT_V = H_A * DV_A
ATT_COLS = 2 * ATT_QK + ATT_V
RW_COLS = 3 * C_R + 2 * R_W + 2 * R_A + R_G
GATE_COLS = 2 * D_MODEL

LANES = 128
SUBLANES = 8
VMEM_LIMIT_BYTES = 56 * 1024 * 1024

SEQ_ALIGN = LANES
ROW_TILE = 512
PREP_ROW_TILE = 256
FF_TILE = 1408
KEY_TILE = LANES
Q_TILE = 256
SCAN_CHUNK = 64
NEG_BIG = -1e30
LOG2E = math.log2(math.e)


def _cparams(*sem):
    return pltpu.CompilerParams(dimension_semantics=sem, vmem_limit_bytes=VMEM_LIMIT_BYTES)


def _rms(x, g):
    return x * lax.rsqrt(jnp.mean(x * x, axis=-1, keepdims=True) + EPS) * g


def _bdot(a, b):
    return jnp.dot(a, b, preferred_element_type=F32)


def _dot_nt(a, b):
    return lax.dot_general(a, b, (((1,), (1,)), ((), ())), preferred_element_type=F32)


def _dot_tn(a, b):
    return lax.dot_general(a, b, (((0,), (0,)), ((), ())), preferred_element_type=F32)


def _ffn_kernel(x_ref, g_ref, wg_ref, wu_ref, wd_ref, fin_ref, o_ref, xn_ref, acc_ref, *, final_norm):
    j = pl.program_id(1)

    @pl.when(j == 0)
    def _():
        xn_ref[...] = _rms(x_ref[...], g_ref[...]).astype(BF16)
        acc_ref[...] = jnp.zeros_like(acc_ref)

    xn = xn_ref[...]
    gate = _bdot(xn, wg_ref[...])
    up = _bdot(xn, wu_ref[...])
    hid = (gate * jax.nn.sigmoid(gate) * up).astype(BF16)
    acc_ref[...] += _bdot(hid, wd_ref[...])

    @pl.when(j == pl.num_programs(1) - 1)
    def _():
        h = x_ref[...] + 0.5 * acc_ref[...]
        if final_norm:
            h = _rms(h, fin_ref[...])
        o_ref[...] = h


def _ffn(x, g, wg, wu, wd, fin, *, final_norm):
    m = x.shape[0]
    tm = min(ROW_TILE, m)
    assert m % tm == 0 and D_FF % FF_TILE == 0
    return pl.pallas_call(
        functools.partial(_ffn_kernel, final_norm=final_norm),
        grid=(m // tm, D_FF // FF_TILE),
        in_specs=[
            pl.BlockSpec((tm, D_MODEL), lambda i, j: (i, 0)),
            pl.BlockSpec((1, D_MODEL), lambda i, j: (0, 0)),
            pl.BlockSpec((D_MODEL, FF_TILE), lambda i, j: (0, j)),
            pl.BlockSpec((D_MODEL, FF_TILE), lambda i, j: (0, j)),
            pl.BlockSpec((FF_TILE, D_MODEL), lambda i, j: (j, 0)),
            pl.BlockSpec((1, D_MODEL), lambda i, j: (0, 0)),
        ],
        out_specs=pl.BlockSpec((tm, D_MODEL), lambda i, j: (i, 0)),
        out_shape=jax.ShapeDtypeStruct((m, D_MODEL), F32),
        scratch_shapes=[pltpu.VMEM((tm, D_MODEL), BF16), pltpu.VMEM((tm, D_MODEL), F32)],
        compiler_params=_cparams("parallel", "arbitrary"),
        name="ffn_final" if final_norm else "ffn",
    )(x, g, wg, wu, wd, fin)


def _proj_kernel(h_ref, g_ref, w_ref, o_ref, *, q_cols):
    u = _rms(h_ref[...], g_ref[...]).astype(BF16)
    p = _bdot(u, w_ref[...])
    if q_cols:
        o_ref[:, :q_cols] = (p[:, :q_cols] * (DH_A ** -0.5 * LOG2E)).astype(o_ref.dtype)
        o_ref[:, q_cols:] = p[:, q_cols:].astype(o_ref.dtype)
    else:
        o_ref[...] = p.astype(o_ref.dtype)


def _proj(h, g, w, out_dtype, q_cols=0, name="proj"):
    m = h.shape[0]
    n = w.shape[1]
    tm = min(ROW_TILE, m)
    assert m % tm == 0
    return pl.pallas_call(
        functools.partial(_proj_kernel, q_cols=q_cols),
        grid=(m // tm,),
        in_specs=[
            pl.BlockSpec((tm, D_MODEL), lambda i: (i, 0)),
            pl.BlockSpec((1, D_MODEL), lambda i: (0, 0)),
            pl.BlockSpec((D_MODEL, n), lambda i: (0, 0)),
        ],
        out_specs=pl.BlockSpec((tm, n), lambda i: (i, 0)),
        out_shape=jax.ShapeDtypeStruct((m, n), out_dtype),
        compiler_params=_cparams("parallel"),
        name=name,
    )(h, g, w)


def _bias_tiles_kernel(tab_ref, o_ref, *, tq):
    j = pl.program_id(0)
    kk = lax.broadcasted_iota(jnp.int32, (KEY_TILE, tq), 0)
    qq = lax.broadcasted_iota(jnp.int32, (KEY_TILE, tq), 1)
    rel = (j - 2) * KEY_TILE + kk - qq
    nb = N_BUCKETS // 2
    max_exact = nb // 2
    n = jnp.abs(rel)
    nf = jnp.maximum(n, 1).astype(F32)
    large = max_exact + (jnp.log(nf / max_exact) / math.log(MAX_DISTANCE / max_exact) * (nb - max_exact)).astype(jnp.int32)
    large = jnp.minimum(large, nb - 1)
    bucket = (rel > 0).astype(jnp.int32) * nb + jnp.where(n < max_exact, n, large)
    for h in range(H_A):
        acc = jnp.zeros((KEY_TILE, tq), F32)
        for b in range(N_BUCKETS):
            acc = jnp.where(bucket == b, tab_ref[b, h], acc)
        o_ref[h, 0] = acc * LOG2E


def _bias_tiles(rel_bias, tq):
    nb = tq // KEY_TILE + 4
    return pl.pallas_call(
        functools.partial(_bias_tiles_kernel, tq=tq),
        grid=(nb,),
        in_specs=[pl.BlockSpec(memory_space=pltpu.SMEM)],
        out_specs=pl.BlockSpec((H_A, 1, KEY_TILE, tq), lambda j: (0, j, 0, 0)),
        out_shape=jax.ShapeDtypeStruct((H_A, nb, KEY_TILE, tq), F32),
        compiler_params=_cparams("arbitrary"),
        name="bias_tiles",
    )(rel_bias)


def _attn_kernel(lq1_ref, lk1_ref, lq2_ref, lk2_ref, q_ref, k_ref, v_ref, bias_ref, g_ref, o_ref,
                 s1_ref, s2_ref, p_ref, vt_ref, *, n_valid, tq, lam_init):
    qi = pl.program_id(2)
    lp = k_ref.shape[1]
    nk = lp // KEY_TILE
    nb = bias_ref.shape[1]
    rows8 = KEY_TILE // SUBLANES

    @pl.when(qi == 0)
    def _():
        for t in range(nk):
            cols = slice(t * KEY_TILE, (t + 1) * KEY_TILE)
            vt_ref[:, cols] = v_ref[0, cols, :].astype(F32).T.astype(BF16)

    lam = (jnp.exp(jnp.sum(lq1_ref[...] * lk1_ref[...], axis=-1, keepdims=True))
           - jnp.exp(jnp.sum(lq2_ref[...] * lk2_ref[...], axis=-1, keepdims=True)) + lam_init)

    q = q_ref[0]
    lane = lax.broadcasted_iota(jnp.int32, q.shape, 1)
    qrow = lax.broadcasted_iota(jnp.int32, q.shape, 0)
    zero = jnp.zeros_like(q)
    in_range = qrow < lp - qi * tq
    q1 = jnp.where(in_range & (lane < DH_A), q, zero)
    q2 = jnp.where(in_range & (lane >= DH_A), q, zero)
    s1_ref[...] = _dot_nt(k_ref[0], q1)
    s2_ref[...] = _dot_nt(k_ref[0], q2)

    def tile_rows(kt):
        return pl.ds(pl.multiple_of(kt * KEY_TILE, KEY_TILE), KEY_TILE)

    def fold(acc, x, op):
        for r8 in range(rows8):
            acc = op(acc, x[r8 * SUBLANES:(r8 + 1) * SUBLANES, :])
        return acc

    def add_bias(kt, carry, n_keys=KEY_TILE):
        m1, m2 = carry
        rows = tile_rows(kt)
        bt = bias_ref[0, jnp.clip(kt - qi * (tq // KEY_TILE) + 2, 0, nb - 1)]
        s1 = s1_ref[rows, :] + bt
        s2 = s2_ref[rows, :] + bt
        if n_keys < KEY_TILE:
            valid = lax.broadcasted_iota(jnp.int32, s1.shape, 0) < n_keys
            s1 = jnp.where(valid, s1, NEG_BIG)
            s2 = jnp.where(valid, s2, NEG_BIG)
        s1_ref[rows, :] = s1
        s2_ref[rows, :] = s2
        return fold(m1, s1, jnp.maximum), fold(m2, s2, jnp.maximum)

    def max_only(kt, carry):
        m1, m2 = carry
        rows = tile_rows(kt)
        return fold(m1, s1_ref[rows, :], jnp.maximum), fold(m2, s2_ref[rows, :], jnp.maximum)

    ratio = tq // KEY_TILE
    n_full = n_valid // KEY_TILE
    lo = jnp.clip(qi * ratio - 1, 0, n_full)
    hi = jnp.clip(qi * ratio + ratio + 1, 0, n_full)
    c_left = bias_ref[0, 0, 0:1, :]
    c_right = bias_ref[0, nb - 1, 0:1, :]
    neg = jnp.full((SUBLANES, tq), NEG_BIG, F32)
    ml1, ml2 = lax.fori_loop(0, lo, max_only, (neg, neg))
    mn1, mn2 = lax.fori_loop(lo, hi, add_bias, (neg, neg))
    mr1, mr2 = lax.fori_loop(hi, n_full, max_only, (neg, neg))
    if n_full < nk:
        mn1, mn2 = add_bias(n_full, (mn1, mn2), n_keys=n_valid - n_full * KEY_TILE)
    colmax = lambda m: jnp.max(m, axis=0, keepdims=True)
    m1 = jnp.maximum(jnp.maximum(colmax(ml1) + c_left, colmax(mr1) + c_right), colmax(mn1))
    m2 = jnp.maximum(jnp.maximum(colmax(ml2) + c_left, colmax(mr2) + c_right), colmax(mn2))

    def exponentiate(shift1, shift2):
        def body(kt, carry):
            l1, l2 = carry
            rows = tile_rows(kt)
            e1 = jnp.exp2(s1_ref[rows, :] - shift1)
            e2 = jnp.exp2(s2_ref[rows, :] - shift2)
            s1_ref[rows, :] = e1
            s2_ref[rows, :] = e2
            return fold(l1, e1, jnp.add), fold(l2, e2, jnp.add)
        return body

    zero8 = jnp.zeros((SUBLANES, tq), F32)
    ls = lax.fori_loop(0, lo, exponentiate(m1 - c_left, m2 - c_left), (zero8, zero8))
    ls = lax.fori_loop(lo, hi, exponentiate(m1, m2), ls)
    ls = lax.fori_loop(hi, n_full, exponentiate(m1 - c_right, m2 - c_right), ls)
    for kt in range(n_full, nk):
        ls = exponentiate(m1, m2)(kt, ls)
    l1, l2 = ls
    inv1 = 1.0 / jnp.sum(l1, axis=0, keepdims=True)
    inv2 = lam / jnp.sum(l2, axis=0, keepdims=True)

    def combine(kt, carry):
        rows = tile_rows(kt)
        p_ref[rows, :] = (s1_ref[rows, :] * inv1 - s2_ref[rows, :] * inv2).astype(BF16)
        return carry

    lax.fori_loop(0, nk, combine, 0, unroll=2)
    acc = _bdot(vt_ref[...], p_ref[...])
    ms = jnp.mean(acc * acc, axis=0, keepdims=True)
    o = acc * lax.rsqrt(ms + EPS) * g_ref[...] * (1.0 - lam_init)
    o_ref[0] = o.T.astype(o_ref.dtype)


def _attention(qkv, bias_tiles, subln_g, lq1, lk1, lq2, lk2, *, n_valid, tq, lam_init):
    b, lp, _ = qkv.shape
    nk = lp // KEY_TILE
    nb = bias_tiles.shape[1]
    k_blk = ATT_QK // LANES
    v_blk = 2 * ATT_QK // LANES
    vec = pl.BlockSpec((1, DH_A), lambda bi, h, qi: (0, 0))
    return pl.pallas_call(
        functools.partial(_attn_kernel, n_valid=n_valid, tq=tq, lam_init=lam_init),
        grid=(b, H_A, pl.cdiv(lp, tq)),
        in_specs=[
            vec, vec, vec, vec,
            pl.BlockSpec((1, tq, LANES), lambda bi, h, qi: (bi, qi, h)),
            pl.BlockSpec((1, lp, LANES), lambda bi, h, qi: (bi, 0, k_blk + h)),
            pl.BlockSpec((1, lp, LANES), lambda bi, h, qi: (bi, 0, v_blk + h)),
            pl.BlockSpec((1, nb, KEY_TILE, tq), lambda bi, h, qi: (h, 0, 0, 0)),
            pl.BlockSpec((DV_A, 1), lambda bi, h, qi: (0, 0)),
        ],
        out_specs=pl.BlockSpec((1, tq, LANES), lambda bi, h, qi: (bi, qi, h)),
        out_shape=jax.ShapeDtypeStruct((b, lp, ATT_V), BF16),
        scratch_shapes=[
            pltpu.VMEM((lp, tq), F32),
            pltpu.VMEM((lp, tq), F32),
            pltpu.VMEM((lp, tq), BF16),
            pltpu.VMEM((DV_A, lp), BF16),
        ],
        compiler_params=_cparams("parallel", "parallel", "arbitrary"),
        name="diff_attention",
    )(lq1, lk1, lq2, lk2, qkv, qkv, qkv, bias_tiles, subln_g)


def _segment_ones(width, scale):
    r = lax.broadcasted_iota(jnp.int32, (width, width), 0) // N_R
    c = lax.broadcasted_iota(jnp.int32, (width, width), 1) // N_R
    return jnp.where(r == c, scale, 0.0).astype(BF16)


def _segment_sum(x, seg):
    width = seg.shape[0]
    outs = []
    for c0 in range(0, x.shape[1], width):
        xs = x[:, c0:c0 + width]
        hi = xs.astype(BF16)
        lo = (xs - hi.astype(F32)).astype(BF16)
        outs.append(_bdot(hi, seg) + _bdot(lo, seg))
    return jnp.concatenate(outs, axis=1)


def _prep_kernel(p_ref, hp_ref, hn_ref, mup_ref, mun_ref, w0_ref, w2_ref, a0_ref, a2_ref, g2_ref,
                 kk_ref, ka_ref, rk_ref,
                 r_ref, v_ref, an_ref, lwf_ref, kdf_ref, bf_ref, lwb_ref, kdb_ref, bb_ref, g_ref, bonus_ref,
                 *, lp, n_valid):
    i = pl.program_id(0)
    tm = p_ref.shape[0]
    p = p_ref[...]
    row = lax.broadcasted_iota(jnp.int32, p.shape, 0)
    prev_row = jnp.where(i == 0, 0.0, hp_ref[SUBLANES - 1:SUBLANES, :])
    next_row = jnp.where(i == pl.num_programs(0) - 1, 0.0, hn_ref[0:1, :])
    prev = jnp.where(row == 0, prev_row, pltpu.roll(p, 1, 0))
    nxt = jnp.where(row == tm - 1, next_row, pltpu.roll(p, tm - 1, 0))
    x = p + mup_ref[...] * (prev - p) + mun_ref[...] * (nxt - p)

    c_wd = 3 * C_R
    c_ad = c_wd + 2 * R_W
    c_gd = c_ad + 2 * R_A
    r = x[:, 0:C_R]
    k = x[:, C_R:2 * C_R]
    v = x[:, 2 * C_R:3 * C_R]
    wd = jnp.tanh(x[:, c_wd:c_ad]).astype(BF16)
    ad = x[:, c_ad:c_gd].astype(BF16)
    gd = jax.nn.sigmoid(x[:, c_gd:c_gd + R_G]).astype(BF16)

    z = w0_ref[...] + _bdot(wd, w2_ref[...])
    softplus = jnp.maximum(-z, 0.0) + jnp.log1p(jnp.exp(-jnp.abs(z)))
    lw = -jnp.exp(-softplus - 0.5)
    a = jax.nn.sigmoid(a0_ref[...] + _bdot(ad, a2_ref[...]))
    g_ref[...] = _bdot(gd, g2_ref[...])

    seg = _segment_ones(2 * LANES, 1.0)
    kk = k * kk_ref[...]
    kk = kk / jnp.maximum(jnp.sqrt(_segment_sum(kk * kk, seg)), 1e-12)

    a_f = a[:, :C_R]
    a_b = a[:, C_R:]
    kd_f = k * (1.0 + (a_f - 1.0) * ka_ref[...])
    kd_b = k * (1.0 + (a_b - 1.0) * ka_ref[...])
    pos = (i * tm + lax.broadcasted_iota(jnp.int32, r.shape, 0)) % lp
    r_ref[...] = r
    v_ref[...] = jnp.where(pos < n_valid, v, 0.0)
    an_ref[...] = -kk
    lwf_ref[...] = lw[:, :C_R]
    lwb_ref[...] = lw[:, C_R:]
    kdf_ref[...] = kd_f
    kdb_ref[...] = kd_b
    bf_ref[...] = kk * a_f
    bb_ref[...] = kk * a_b
    bonus_ref[...] = _segment_sum(r * rk_ref[...] * (kd_f + kd_b), seg) * v


def _rwkv_prep(p_rw, mu_prev, mu_next, w0, w2, a0, a2, g2, k_k, k_a, r_k, *, lp, n_valid):
    m = p_rw.shape[0]
    tm = min(PREP_ROW_TILE, m)
    assert m % tm == 0
    h8 = tm // SUBLANES
    last8 = m // SUBLANES - 1
    full = lambda shape: pl.BlockSpec(shape, lambda i: tuple(0 for _ in shape))
    row_out = pl.BlockSpec((tm, C_R), lambda i: (i, 0))
    return pl.pallas_call(
        functools.partial(_prep_kernel, lp=lp, n_valid=n_valid),
        grid=(m // tm,),
        in_specs=[
            pl.BlockSpec((tm, RW_COLS), lambda i: (i, 0)),
            pl.BlockSpec((SUBLANES, RW_COLS), lambda i: (jnp.maximum(i * h8 - 1, 0), 0)),
            pl.BlockSpec((SUBLANES, RW_COLS), lambda i: (jnp.minimum((i + 1) * h8, last8), 0)),
            full((1, RW_COLS)), full((1, RW_COLS)),
            full((1, 2 * C_R)), full((2 * R_W, 2 * C_R)),
            full((1, 2 * C_R)), full((2 * R_A, 2 * C_R)),
            full((R_G, C_R)),
            full((1, C_R)), full((1, C_R)), full((1, C_R)),
        ],
        out_specs=[row_out] * 11,
        out_shape=[jax.ShapeDtypeStruct((m, C_R), F32)] * 11,
        compiler_params=_cparams("parallel"),
        name="rwkv_prep",
    )(p_rw, p_rw, p_rw, mu_prev, mu_next, w0, w2, a0, a2, g2, k_k, k_a, r_k)


def _scan_kernel(r_ref, lw_ref, kd_ref, v_ref, an_ref, b_ref, y_ref,
                 s_ref, lhs_ref, tav_ref, arb_ref, arkv_ref, bt_ref, vtk_ref, wtot_ref, *, reverse):
    c = SCAN_CHUNK
    c2 = 2 * c
    tb = r_ref.shape[1]
    n_pairs = r_ref.shape[2] // LANES

    @pl.when(pl.program_id(1) == 0)
    def _():
        s_ref[...] = jnp.zeros_like(s_ref)

    def before(s, t, strict):
        if reverse:
            return (s > t) if strict else (s >= t)
        return (s < t) if strict else (s <= t)

    rc = lax.broadcasted_iota(jnp.int32, (c, c), 0)
    cc = lax.broadcasted_iota(jnp.int32, (c, c), 1)
    cum_mat = jnp.where(before(cc, rc, False), 1.0, 0.0).astype(BF16)
    r2 = lax.broadcasted_iota(jnp.int32, (c2, c2), 0)
    q2 = lax.broadcasted_iota(jnp.int32, (c2, c2), 1)
    same_head = (r2 >= c) == (q2 >= c)
    t2 = jnp.where(r2 >= c, r2 - c, r2)
    u2 = jnp.where(q2 >= c, q2 - c, q2)
    strict = same_head & before(u2, t2, True)
    incl = same_head & before(u2, t2, False)
    pair_mask = ((lax.broadcasted_iota(jnp.int32, (c2, LANES), 0) >= c)
                 == (lax.broadcasted_iota(jnp.int32, (c2, LANES), 1) >= N_R))
    last_row = 0 if reverse else c - 1

    def stack(x):
        return jnp.where(pair_mask, jnp.concatenate([x, x], axis=0), 0.0)

    eye = jnp.where(r2 == q2, 1.0, 0.0)
    steps = c.bit_length() - 1

    pairs = range(n_pairs)
    lane_slices = [slice(p * LANES, (p + 1) * LANES) for p in pairs]

    def prepare_chunk(ci, carry):
        rows = pl.ds(pl.multiple_of(ci * c, c), c)
        lw = lw_ref[0, rows, :]
        hi = lw.astype(BF16)
        rem = lw - hi.astype(F32)
        mid = rem.astype(BF16)
        lo = (rem - mid.astype(F32)).astype(BF16)
        cl = _bdot(cum_mat, hi) + _bdot(cum_mat, mid) + _bdot(cum_mat, lo)
        e_neg = jnp.exp(-cl)
        ab_all = an_ref[0, rows, :] * jnp.exp(cl - lw)
        rb_all = r_ref[0, rows, :] * jnp.exp(cl)
        bt_all = b_ref[0, rows, :] * e_neg
        kt_all = kd_ref[0, rows, :] * e_neg
        v_all = v_ref[0, rows, :]
        w_tot = jnp.exp(cl[last_row:last_row + 1, :])

        stacked = lambda x: [stack(x[:, ls]).astype(BF16) for ls in lane_slices]
        ab, rb, bt, kt, vs = stacked(ab_all), stacked(rb_all), stacked(bt_all), stacked(kt_all), stacked(v_all)
        gram = [_dot_nt(jnp.concatenate([ab[p], rb[p]], axis=0), jnp.concatenate([bt[p], kt[p]], axis=0))
                for p in pairs]
        n_ab = [jnp.where(strict, g[:c2, :c2], 0.0) for g in gram]
        a_ak = [jnp.where(strict, g[:c2, c2:], 0.0) for g in gram]
        a_rb = [jnp.where(incl, g[c2:, :c2], 0.0).astype(BF16) for g in gram]
        a_rk = [jnp.where(incl, g[c2:, c2:], 0.0) for g in gram]
        akv = [_bdot(jnp.concatenate([a_ak[p], a_rk[p]], axis=0).astype(BF16), vs[p]) for p in pairs]
        vtk = [_dot_tn(vs[p], kt[p]) for p in pairs]

        t_mat = [eye + n for n in n_ab]
        pw = [n.astype(BF16) for n in n_ab]
        for _ in range(1, steps):
            pw = [_bdot(x, x).astype(BF16) for x in pw]
            t_mat = [t_mat[p] + _bdot(t_mat[p].astype(BF16), pw[p]) for p in pairs]
        applied = [_bdot(t_mat[p].astype(BF16), jnp.concatenate([ab[p], akv[p][:c2].astype(BF16)], axis=1))
                   for p in pairs]
        for p in pairs:
            lhs_ref[ci, p] = jnp.concatenate([applied[p][:, :LANES].astype(BF16), rb[p]], axis=0)
            tav_ref[ci, p] = applied[p][:, LANES:]
            arb_ref[ci, p] = a_rb[p]
            arkv_ref[ci, p] = akv[p][c2:]
            bt_ref[ci, p] = bt[p]
            vtk_ref[ci, p] = vtk[p]
            wtot_ref[ci, p] = jnp.broadcast_to(w_tot[:, lane_slices[p]], (SUBLANES, LANES))
        return carry

    n_chunks = tb // c

    def advance_chunk(i, carry):
        ci = (n_chunks - 1 - i) if reverse else i
        rows = pl.ds(pl.multiple_of(ci * c, c), c)
        s0 = [s_ref[p] for p in pairs]
        xs = [_dot_nt(lhs_ref[ci, p], s0[p].astype(BF16)) for p in pairs]
        u = [(xs[p][:c2] + tav_ref[ci, p]).astype(BF16) for p in pairs]
        du = [_dot_tn(u[p], bt_ref[ci, p]) for p in pairs]
        yu = [_bdot(arb_ref[ci, p], u[p]) for p in pairs]
        for p in pairs:
            s_ref[p] = (s0[p] + du[p] + vtk_ref[ci, p]) * wtot_ref[ci, p][0:1, :]
            ys = xs[p][c2:] + yu[p] + arkv_ref[ci, p]
            y_ref[0, rows, lane_slices[p]] = ys[:c] + ys[c:]
        return carry

    lax.fori_loop(0, n_chunks, prepare_chunk, 0)
    lax.fori_loop(0, n_chunks, advance_chunk, 0)


def _wkv_scan(r, lw, kd, v, an, b, *, reverse, tb):
    bsz, lp, _ = r.shape
    nt = lp // tb
    assert lp % tb == 0 and tb % SCAN_CHUNK == 0
    n_chunks = tb // SCAN_CHUNK
    n_pairs = C_R // LANES
    c2 = 2 * SCAN_CHUNK
    tmap = (lambda bi, t: (bi, nt - 1 - t, 0)) if reverse else (lambda bi, t: (bi, t, 0))
    spec = pl.BlockSpec((1, tb, C_R), tmap)
    per_chunk = lambda rows, dtype: pltpu.VMEM((n_chunks, n_pairs, rows, LANES), dtype)
    return pl.pallas_call(
        functools.partial(_scan_kernel, reverse=reverse),
        grid=(bsz, nt),
        in_specs=[spec] * 6,
        out_specs=spec,
        out_shape=jax.ShapeDtypeStruct((bsz, lp, C_R), F32),
        scratch_shapes=[
            pltpu.VMEM((n_pairs, LANES, LANES), F32),
            per_chunk(2 * c2, BF16),
            per_chunk(c2, F32),
            per_chunk(c2, BF16),
            per_chunk(c2, F32),
            per_chunk(c2, BF16),
            per_chunk(LANES, F32),
            per_chunk(SUBLANES, F32),
        ],
        compiler_params=_cparams("parallel", "arbitrary"),
        name="wkv_scan_bwd" if reverse else "wkv_scan_fwd",
    )(r, lw, kd, v, an, b)


def _merge_kernel(yf_ref, yb_ref, bonus_ref, g_ref, o_ref, gate_ref, h_ref, lnw_ref, lnb_ref,
                  wrw_ref, watt_ref, wout_ref, out_ref):
    seg = _segment_ones(2 * LANES, 1.0 / N_R)
    y = yf_ref[...] + yb_ref[...]
    d = y - _segment_sum(y, seg)
    var = _segment_sum(d * d, seg)
    yn = d * lax.rsqrt(var + LNX_EPS) * lnw_ref[...] + lnb_ref[...]
    y_rw = _bdot(((yn + bonus_ref[...]) * g_ref[...]).astype(BF16), wrw_ref[...])
    y_att = _bdot(o_ref[...], watt_ref[...])
    gs = jax.nn.sigmoid(gate_ref[...])
    merged = gs[:, :D_MODEL] * y_att + gs[:, D_MODEL:] * y_rw
    out_ref[...] = h_ref[...] + _bdot(merged.astype(BF16), wout_ref[...])


def _merge(yf, yb, bonus, g, o, gate, h, lnw, lnb, wrw, watt, wout):
    m = h.shape[0]
    tm = min(PREP_ROW_TILE, m)
    assert m % tm == 0
    rows = lambda n: pl.BlockSpec((tm, n), lambda i: (i, 0))
    full = lambda shape: pl.BlockSpec(shape, lambda i: tuple(0 for _ in shape))
    return pl.pallas_call(
        _merge_kernel,
        grid=(m // tm,),
        in_specs=[rows(C_R), rows(C_R), rows(C_R), rows(C_R), rows(ATT_V), rows(GATE_COLS), rows(D_MODEL),
                  full((1, C_R)), full((1, C_R)),
                  full((C_R, D_MODEL)), full((ATT_V, D_MODEL)), full((D_MODEL, D_MODEL))],
        out_specs=rows(D_MODEL),
        out_shape=jax.ShapeDtypeStruct((m, D_MODEL), F32),
        compiler_params=_cparams("parallel"),
        name="merge",
    )(yf, yb, bonus, g, o, gate, h, lnw, lnb, wrw, watt, wout)


def _pair_block_diag(w):
    z = jnp.zeros_like(w[0])
    return jnp.concatenate([jnp.concatenate([w[0], z], axis=1), jnp.concatenate([z, w[1]], axis=1)], axis=0)


def _scan_time_block(lp):
    for tb in (512, 384, 256, 128):
        if lp % tb == 0:
            return tb
    raise ValueError(lp)


def _encode(x, P, bias_tiles):
    b, seq, _ = x.shape
    n_valid = N_META + seq
    lp = (n_valid // SEQ_ALIGN + 1) * SEQ_ALIGN
    m = b * lp
    meta = jnp.broadcast_to(P["meta_tokens"].astype(x.dtype)[None], (b, N_META, D_MODEL))
    h0 = jnp.concatenate([meta, x, jnp.zeros((b, lp - n_valid, D_MODEL), x.dtype)], axis=1).reshape(m, D_MODEL)

    row = lambda a: a.reshape(1, -1)
    bf = lambda a: a.astype(BF16)
    l = 0
    h1 = _ffn(h0, row(P["ffn1_norm"][l]), bf(P["ffn1_w_gate"][l]), bf(P["ffn1_w_up"][l]), bf(P["ffn1_w_down"][l]),
              row(P["final_norm"]), final_norm=False)

    w_in = bf(P["w_in"][l])
    g_mix = row(P["mix_norm"][l])
    qkv = _proj(h1, g_mix, w_in[:, :ATT_COLS], BF16, q_cols=ATT_QK, name="proj_qkv")
    p_rw = _proj(h1, g_mix, w_in[:, ATT_COLS:ATT_COLS + RW_COLS], F32, name="proj_rw")
    p_gate = _proj(h1, g_mix, w_in[:, ATT_COLS + RW_COLS:], F32, name="proj_gate")

    lam_init = 0.8 - 0.6 * math.exp(-0.3 * l)
    o_att = _attention(qkv.reshape(b, lp, ATT_COLS), bias_tiles, P["attn_subln"][l].reshape(DV_A, 1),
                       row(P["attn_lambda_q1"][l]), row(P["attn_lambda_k1"][l]),
                       row(P["attn_lambda_q2"][l]), row(P["attn_lambda_k2"][l]),
                       n_valid=n_valid, tq=Q_TILE, lam_init=lam_init).reshape(m, ATT_V)

    (r, v, an, lw_f, kd_f, b_f, lw_b, kd_b, b_b, g, bonus) = _rwkv_prep(
        p_rw, row(P["rw_mu_prev"][l]), row(P["rw_mu_next"][l]),
        row(P["rw_w0"][l]), bf(_pair_block_diag(P["rw_w2"][l])),
        row(P["rw_a0"][l]), bf(_pair_block_diag(P["rw_a2"][l])),
        bf(P["rw_g2"][l]), row(P["rw_k_k"][l]), row(P["rw_k_a"][l]), row(P["rw_r_k"][l]),
        lp=lp, n_valid=n_valid)
    seq3 = lambda a: a.reshape(b, lp, C_R)
    tb = _scan_time_block(lp)
    y_f = _wkv_scan(seq3(r), seq3(lw_f), seq3(kd_f), seq3(v), seq3(an), seq3(b_f), reverse=False, tb=tb)
    y_b = _wkv_scan(seq3(r), seq3(lw_b), seq3(kd_b), seq3(v), seq3(an), seq3(b_b), reverse=True, tb=tb)

    h2 = _merge(y_f.reshape(m, C_R), y_b.reshape(m, C_R), bonus, g, o_att, p_gate, h1,
                row(P["rw_lnx_w"][l]), row(P["rw_lnx_b"][l]),
                bf(P["w_rw_branch"][l]), bf(P["w_attn_branch"][l]), bf(P["w_out"][l]))
    out = _ffn(h2, row(P["ffn2_norm"][l]), bf(P["ffn2_w_gate"][l]), bf(P["ffn2_w_up"][l]), bf(P["ffn2_w_down"][l]),
               row(P["final_norm"]), final_norm=True)
    return out.reshape(b, lp, D_MODEL)[:, N_META:n_valid]


def kernel(x_prompt, x_sample, meta_tokens, rel_bias, ffn1_norm, ffn1_w_gate, ffn1_w_up, ffn1_w_down,
           mix_norm, w_in, attn_lambda_q1, attn_lambda_k1, attn_lambda_q2, attn_lambda_k2, attn_subln,
           w_attn_branch, rw_mu_prev, rw_mu_next, rw_w0, rw_w2, rw_a0, rw_a2, rw_g2, rw_k_k, rw_k_a,
           rw_r_k, rw_lnx_w, rw_lnx_b, w_rw_branch, w_out, ffn2_norm, ffn2_w_gate, ffn2_w_up,
           ffn2_w_down, final_norm):
    assert ffn1_norm.shape[0] == 1, "single layer"
    P = dict(meta_tokens=meta_tokens, ffn1_norm=ffn1_norm, ffn1_w_gate=ffn1_w_gate,
             ffn1_w_up=ffn1_w_up, ffn1_w_down=ffn1_w_down, mix_norm=mix_norm, w_in=w_in,
             attn_lambda_q1=attn_lambda_q1, attn_lambda_k1=attn_lambda_k1, attn_lambda_q2=attn_lambda_q2,
             attn_lambda_k2=attn_lambda_k2, attn_subln=attn_subln, w_attn_branch=w_attn_branch,
             rw_mu_prev=rw_mu_prev, rw_mu_next=rw_mu_next, rw_w0=rw_w0, rw_w2=rw_w2, rw_a0=rw_a0,
             rw_a2=rw_a2, rw_g2=rw_g2, rw_k_k=rw_k_k, rw_k_a=rw_k_a, rw_r_k=rw_r_k, rw_lnx_w=rw_lnx_w,
             rw_lnx_b=rw_lnx_b, w_rw_branch=w_rw_branch, w_out=w_out, ffn2_norm=ffn2_norm,
             ffn2_w_gate=ffn2_w_gate, ffn2_w_up=ffn2_w_up, ffn2_w_down=ffn2_w_down, final_norm=final_norm)
    bias_tiles = _bias_tiles(rel_bias, Q_TILE)
    return (_encode(x_prompt, P, bias_tiles), _encode(x_sample, P, bias_tiles))
```

```python
import functools
import math

import jax
import jax.numpy as jnp
from jax import lax
from jax.experimental import pallas as pl
from jax.experimental.pallas import tpu as pltpu

F32 = jnp.float32
BF16 = jnp.bfloat16

D_MODEL = 1024
N_META = 16
D_FF = 2816
EPS = 1e-6
H_A = 8
DH_A = 64
DV_A = 2 * DH_A
N_BUCKETS = 32
MAX_DISTANCE = 128
H_R = 16
N_R = 64
C_R = H_R * N_R
R_W = 64
R_A = 64
R_G = 128
LNX_EPS = 64e-5
ATT_QK = H_A * 2 * DH_A
ATT_V = H_A * DV_A
ATT_COLS = 2 * ATT_QK + ATT_V
RW_COLS = 3 * C_R + 2 * R_W + 2 * R_A + R_G
GATE_COLS = 2 * D_MODEL

LANES = 128
SUBLANES = 8
VMEM_LIMIT_BYTES = 56 * 1024 * 1024

SEQ_ALIGN = LANES
ROW_TILE = 512
PREP_ROW_TILE = 256
FF_TILE = 1408
KEY_TILE = LANES
Q_TILE = 256
SCAN_CHUNK = 64
NEG_BIG = -1e30
LOG2E = math.log2(math.e)


def _cparams(*sem):
    return pltpu.CompilerParams(dimension_semantics=sem, vmem_limit_bytes=VMEM_LIMIT_BYTES)


def _rms(x, g):
    return x * lax.rsqrt(jnp.mean(x * x, axis=-1, keepdims=True) + EPS) * g


def _bdot(a, b):
    return jnp.dot(a, b, preferred_element_type=F32)


def _dot_nt(a, b):
    return lax.dot_general(a, b, (((1,), (1,)), ((), ())), preferred_element_type=F32)


def _dot_tn(a, b):
    return lax.dot_general(a, b, (((0,), (0,)), ((), ())), preferred_element_type=F32)


def _ffn_kernel(x_ref, g_ref, wg_ref, wu_ref, wd_ref, fin_ref, o_ref, xn_ref, acc_ref, *, final_norm):
    j = pl.program_id(1)

    @pl.when(j == 0)
    def _():
        xn_ref[...] = _rms(x_ref[...], g_ref[...]).astype(BF16)
        acc_ref[...] = jnp.zeros_like(acc_ref)

    xn = xn_ref[...]
    gate = _bdot(xn, wg_ref[...])
    up = _bdot(xn, wu_ref[...])
    hid = (gate * jax.nn.sigmoid(gate) * up).astype(BF16)
    acc_ref[...] += _bdot(hid, wd_ref[...])

    @pl.when(j == pl.num_programs(1) - 1)
    def _():
        h = x_ref[...] + 0.5 * acc_ref[...]
        if final_norm:
            h = _rms(h, fin_ref[...])
        o_ref[...] = h


def _ffn(x, g, wg, wu, wd, fin, *, final_norm):
    m = x.shape[0]
    tm = min(ROW_TILE, m)
    assert m % tm == 0 and D_FF % FF_TILE == 0
    return pl.pallas_call(
        functools.partial(_ffn_kernel, final_norm=final_norm),
        grid=(m // tm, D_FF // FF_TILE),
        in_specs=[
            pl.BlockSpec((tm, D_MODEL), lambda i, j: (i, 0)),
            pl.BlockSpec((1, D_MODEL), lambda i, j: (0, 0)),
            pl.BlockSpec((D_MODEL, FF_TILE), lambda i, j: (0, j)),
            pl.BlockSpec((D_MODEL, FF_TILE), lambda i, j: (0, j)),
            pl.BlockSpec((FF_TILE, D_MODEL), lambda i, j: (j, 0)),
            pl.BlockSpec((1, D_MODEL), lambda i, j: (0, 0)),
        ],
        out_specs=pl.BlockSpec((tm, D_MODEL), lambda i, j: (i, 0)),
        out_shape=jax.ShapeDtypeStruct((m, D_MODEL), F32),
        scratch_shapes=[pltpu.VMEM((tm, D_MODEL), BF16), pltpu.VMEM((tm, D_MODEL), F32)],
        compiler_params=_cparams("parallel", "arbitrary"),
        name="ffn_final" if final_norm else "ffn",
    )(x, g, wg, wu, wd, fin)


def _proj_kernel(h_ref, g_ref, w_ref, o_ref, *, q_cols):
    u = _rms(h_ref[...], g_ref[...]).astype(BF16)
    p = _bdot(u, w_ref[...])
    if q_cols:
        o_ref[:, :q_cols] = (p[:, :q_cols] * (DH_A ** -0.5 * LOG2E)).astype(o_ref.dtype)
        o_ref[:, q_cols:] = p[:, q_cols:].astype(o_ref.dtype)
    else:
        o_ref[...] = p.astype(o_ref.dtype)


def _proj(h, g, w, out_dtype, q_cols=0, name="proj"):
    m = h.shape[0]
    n = w.shape[1]
    tm = min(ROW_TILE, m)
    assert m % tm == 0
    return pl.pallas_call(
        functools.partial(_proj_kernel, q_cols=q_cols),
        grid=(m // tm,),
        in_specs=[
            pl.BlockSpec((tm, D_MODEL), lambda i: (i, 0)),
            pl.BlockSpec((1, D_MODEL), lambda i: (0, 0)),
            pl.BlockSpec((D_MODEL, n), lambda i: (0, 0)),
        ],
        out_specs=pl.BlockSpec((tm, n), lambda i: (i, 0)),
        out_shape=jax.ShapeDtypeStruct((m, n), out_dtype),
        compiler_params=_cparams("parallel"),
        name=name,
    )(h, g, w)


def _bias_tiles_kernel(tab_ref, o_ref, *, tq):
    j = pl.program_id(0)
    kk = lax.broadcasted_iota(jnp.int32, (KEY_TILE, tq), 0)
    qq = lax.broadcasted_iota(jnp.int32, (KEY_TILE, tq), 1)
    rel = (j - 2) * KEY_TILE + kk - qq
    nb = N_BUCKETS // 2
    max_exact = nb // 2
    n = jnp.abs(rel)
    nf = jnp.maximum(n, 1).astype(F32)
    large = max_exact + (jnp.log(nf / max_exact) / math.log(MAX_DISTANCE / max_exact) * (nb - max_exact)).astype(jnp.int32)
    large = jnp.minimum(large, nb - 1)
    bucket = (rel > 0).astype(jnp.int32) * nb + jnp.where(n < max_exact, n, large)
    for h in range(H_A):
        acc = jnp.zeros((KEY_TILE, tq), F32)
        for b in range(N_BUCKETS):
            acc = jnp.where(bucket == b, tab_ref[b, h], acc)
        o_ref[h, 0] = acc * LOG2E


def _bias_tiles(rel_bias, tq):
    nb = tq // KEY_TILE + 4
    return pl.pallas_call(
        functools.partial(_bias_tiles_kernel, tq=tq),
        grid=(nb,),
        in_specs=[pl.BlockSpec(memory_space=pltpu.SMEM)],
        out_specs=pl.BlockSpec((H_A, 1, KEY_TILE, tq), lambda j: (0, j, 0, 0)),
        out_shape=jax.ShapeDtypeStruct((H_A, nb, KEY_TILE, tq), F32),
        compiler_params=_cparams("arbitrary"),
        name="bias_tiles",
    )(rel_bias)


def _attn_kernel(lq1_ref, lk1_ref, lq2_ref, lk2_ref, q_ref, k_ref, v_ref, bias_ref, g_ref, o_ref,
                 s1_ref, s2_ref, p_ref, vt_ref, *, n_valid, tq, lam_init):
    qi = pl.program_id(2)
    lp = k_ref.shape[1]
    nk = lp // KEY_TILE
    nb = bias_ref.shape[1]
    rows8 = KEY_TILE // SUBLANES

    @pl.when(qi == 0)
    def _():
        for t in range(nk):
            cols = slice(t * KEY_TILE, (t + 1) * KEY_TILE)
            vt_ref[:, cols] = v_ref[0, cols, :].astype(F32).T.astype(BF16)

    lam = (jnp.exp(jnp.sum(lq1_ref[...] * lk1_ref[...], axis=-1, keepdims=True))
           - jnp.exp(jnp.sum(lq2_ref[...] * lk2_ref[...], axis=-1, keepdims=True)) + lam_init)

    q = q_ref[0]
    lane = lax.broadcasted_iota(jnp.int32, q.shape, 1)
    qrow = lax.broadcasted_iota(jnp.int32, q.shape, 0)
    zero = jnp.zeros_like(q)
    in_range = qrow < lp - qi * tq
    q1 = jnp.where(in_range & (lane < DH_A), q, zero)
    q2 = jnp.where(in_range & (lane >= DH_A), q, zero)
    s1_ref[...] = _dot_nt(k_ref[0], q1)
    s2_ref[...] = _dot_nt(k_ref[0], q2)

    def tile_rows(kt):
        return pl.ds(pl.multiple_of(kt * KEY_TILE, KEY_TILE), KEY_TILE)

    def fold(acc, x, op):
        for r8 in range(rows8):
            acc = op(acc, x[r8 * SUBLANES:(r8 + 1) * SUBLANES, :])
        return acc

    def add_bias(kt, carry, n_keys=KEY_TILE):
        m1, m2 = carry
        rows = tile_rows(kt)
        bt = bias_ref[0, jnp.clip(kt - qi * (tq // KEY_TILE) + 2, 0, nb - 1)]
        s1 = s1_ref[rows, :] + bt
        s2 = s2_ref[rows, :] + bt
        if n_keys < KEY_TILE:
            valid = lax.broadcasted_iota(jnp.int32, s1.shape, 0) < n_keys
            s1 = jnp.where(valid, s1, NEG_BIG)
            s2 = jnp.where(valid, s2, NEG_BIG)
        s1_ref[rows, :] = s1
        s2_ref[rows, :] = s2
        return fold(m1, s1, jnp.maximum), fold(m2, s2, jnp.maximum)

    def max_only(kt, carry):
        m1, m2 = carry
        rows = tile_rows(kt)
        return fold(m1, s1_ref[rows, :], jnp.maximum), fold(m2, s2_ref[rows, :], jnp.maximum)

    ratio = tq // KEY_TILE
    n_full = n_valid // KEY_TILE
    lo = jnp.clip(qi * ratio - 1, 0, n_full)
    hi = jnp.clip(qi * ratio + ratio + 1, 0, n_full)
    c_left = bias_ref[0, 0, 0:1, :]
    c_right = bias_ref[0, nb - 1, 0:1, :]
    neg = jnp.full((SUBLANES, tq), NEG_BIG, F32)
    ml1, ml2 = lax.fori_loop(0, lo, max_only, (neg, neg))
    mn1, mn2 = lax.fori_loop(lo, hi, add_bias, (neg, neg))
    mr1, mr2 = lax.fori_loop(hi, n_full, max_only, (neg, neg))
    if n_full < nk:
        mn1, mn2 = add_bias(n_full, (mn1, mn2), n_keys=n_valid - n_full * KEY_TILE)
    colmax = lambda m: jnp.max(m, axis=0, keepdims=True)
    m1 = jnp.maximum(jnp.maximum(colmax(ml1) + c_left, colmax(mr1) + c_right), colmax(mn1))
    m2 = jnp.maximum(jnp.maximum(colmax(ml2) + c_left, colmax(mr2) + c_right), colmax(mn2))

    def exponentiate(shift1, shift2):
        def body(kt, carry):
            l1, l2 = carry
            rows = tile_rows(kt)
            e1 = jnp.exp2(s1_ref[rows, :] - shift1)
            e2 = jnp.exp2(s2_ref[rows, :] - shift2)
            s1_ref[rows, :] = e1
            s2_ref[rows, :] = e2
            return fold(l1, e1, jnp.add), fold(l2, e2, jnp.add)
        return body

    zero8 = jnp.zeros((SUBLANES, tq), F32)
    ls = lax.fori_loop(0, lo, exponentiate(m1 - c_left, m2 - c_left), (zero8, zero8))
    ls = lax.fori_loop(lo, hi, exponentiate(m1, m2), ls)
    ls = lax.fori_loop(hi, n_full, exponentiate(m1 - c_right, m2 - c_right), ls)
    for kt in range(n_full, nk):
        ls = exponentiate(m1, m2)(kt, ls)
    l1, l2 = ls
    inv1 = 1.0 / jnp.sum(l1, axis=0, keepdims=True)
    inv2 = lam / jnp.sum(l2, axis=0, keepdims=True)

    def combine(kt, carry):
        rows = tile_rows(kt)
        p_ref[rows, :] = (s1_ref[rows, :] * inv1 - s2_ref[rows, :] * inv2).astype(BF16)
        return carry

    lax.fori_loop(0, nk, combine, 0, unroll=2)
    acc = _bdot(vt_ref[...], p_ref[...])
    ms = jnp.mean(acc * acc, axis=0, keepdims=True)
    o = acc * lax.rsqrt(ms + EPS) * g_ref[...] * (1.0 - lam_init)
    o_ref[0] = o.T.astype(o_ref.dtype)


def _attention(qkv, bias_tiles, subln_g, lq1, lk1, lq2, lk2, *, n_valid, tq, lam_init):
    b, lp, _ = qkv.shape
    nk = lp // KEY_TILE
    nb = bias_tiles.shape[1]
    k_blk = ATT_QK // LANES
    v_blk = 2 * ATT_QK // LANES
    vec = pl.BlockSpec((1, DH_A), lambda bi, h, qi: (0, 0))
    return pl.pallas_call(
        functools.partial(_attn_kernel, n_valid=n_valid, tq=tq, lam_init=lam_init),
        grid=(b, H_A, pl.cdiv(lp, tq)),
        in_specs=[
            vec, vec, vec, vec,
            pl.BlockSpec((1, tq, LANES), lambda bi, h, qi: (bi, qi, h)),
            pl.BlockSpec((1, lp, LANES), lambda bi, h, qi: (bi, 0, k_blk + h)),
            pl.BlockSpec((1, lp, LANES), lambda bi, h, qi: (bi, 0, v_blk + h)),
            pl.BlockSpec((1, nb, KEY_TILE, tq), lambda bi, h, qi: (h, 0, 0, 0)),
            pl.BlockSpec((DV_A, 1), lambda bi, h, qi: (0, 0)),
        ],
        out_specs=pl.BlockSpec((1, tq, LANES), lambda bi, h, qi: (bi, qi, h)),
        out_shape=jax.ShapeDtypeStruct((b, lp, ATT_V), BF16),
        scratch_shapes=[
            pltpu.VMEM((lp, tq), F32),
            pltpu.VMEM((lp, tq), F32),
            pltpu.VMEM((lp, tq), BF16),
            pltpu.VMEM((DV_A, lp), BF16),
        ],
        compiler_params=_cparams("parallel", "parallel", "arbitrary"),
        name="diff_attention",
    )(lq1, lk1, lq2, lk2, qkv, qkv, qkv, bias_tiles, subln_g)


def _segment_ones(width, scale):
    r = lax.broadcasted_iota(jnp.int32, (width, width), 0) // N_R
    c = lax.broadcasted_iota(jnp.int32, (width, width), 1) // N_R
    return jnp.where(r == c, scale, 0.0).astype(BF16)


def _segment_sum(x, seg):
    width = seg.shape[0]
    outs = []
    for c0 in range(0, x.shape[1], width):
        xs = x[:, c0:c0 + width]
        hi = xs.astype(BF16)
        lo = (xs - hi.astype(F32)).astype(BF16)
        outs.append(_bdot(hi, seg) + _bdot(lo, seg))
    return jnp.concatenate(outs, axis=1)


def _prep_kernel(p_ref, hp_ref, hn_ref, mup_ref, mun_ref, w0_ref, w2_ref, a0_ref, a2_ref, g2_ref,
                 kk_ref, ka_ref, rk_ref,
                 r_ref, v_ref, an_ref, lwf_ref, kdf_ref, bf_ref, lwb_ref, kdb_ref, bb_ref, g_ref, bonus_ref,
                 *, lp, n_valid):
    i = pl.program_id(0)
    tm = p_ref.shape[0]
    p = p_ref[...]
    row = lax.broadcasted_iota(jnp.int32, p.shape, 0)
    prev_row = jnp.where(i == 0, 0.0, hp_ref[SUBLANES - 1:SUBLANES, :])
    next_row = jnp.where(i == pl.num_programs(0) - 1, 0.0, hn_ref[0:1, :])
    prev = jnp.where(row == 0, prev_row, pltpu.roll(p, 1, 0))
    nxt = jnp.where(row == tm - 1, next_row, pltpu.roll(p, tm - 1, 0))
    x = p + mup_ref[...] * (prev - p) + mun_ref[...] * (nxt - p)

    c_wd = 3 * C_R
    c_ad = c_wd + 2 * R_W
    c_gd = c_ad + 2 * R_A
    r = x[:, 0:C_R]
    k = x[:, C_R:2 * C_R]
    v = x[:, 2 * C_R:3 * C_R]
    wd = jnp.tanh(x[:, c_wd:c_ad]).astype(BF16)
    ad = x[:, c_ad:c_gd].astype(BF16)
    gd = jax.nn.sigmoid(x[:, c_gd:c_gd + R_G]).astype(BF16)

    z = w0_ref[...] + _bdot(wd, w2_ref[...])
    softplus = jnp.maximum(-z, 0.0) + jnp.log1p(jnp.exp(-jnp.abs(z)))
    lw = -jnp.exp(-softplus - 0.5)
    a = jax.nn.sigmoid(a0_ref[...] + _bdot(ad, a2_ref[...]))
    g_ref[...] = _bdot(gd, g2_ref[...]).astype(g_ref.dtype)

    seg = _segment_ones(2 * LANES, 1.0)
    kk = k * kk_ref[...]
    kk = kk / jnp.maximum(jnp.sqrt(_segment_sum(kk * kk, seg)), 1e-12)

    a_f = a[:, :C_R]
    a_b = a[:, C_R:]
    kd_f = k * (1.0 + (a_f - 1.0) * ka_ref[...])
    kd_b = k * (1.0 + (a_b - 1.0) * ka_ref[...])
    pos = (i * tm + lax.broadcasted_iota(jnp.int32, r.shape, 0)) % lp
    r_ref[...] = r.astype(r_ref.dtype)
    v_ref[...] = jnp.where(pos < n_valid, v, 0.0).astype(v_ref.dtype)
    an_ref[...] = (-kk).astype(an_ref.dtype)
    lwf_ref[...] = lw[:, :C_R]
    lwb_ref[...] = lw[:, C_R:]
    kdf_ref[...] = kd_f.astype(kdf_ref.dtype)
    kdb_ref[...] = kd_b.astype(kdb_ref.dtype)
    bf_ref[...] = (kk * a_f).astype(bf_ref.dtype)
    bb_ref[...] = (kk * a_b).astype(bb_ref.dtype)
    bonus_ref[...] = (_segment_sum(r * rk_ref[...] * (kd_f + kd_b), seg) * v).astype(bonus_ref.dtype)


def _rwkv_prep(p_rw, mu_prev, mu_next, w0, w2, a0, a2, g2, k_k, k_a, r_k, *, lp, n_valid):
    m = p_rw.shape[0]
    tm = min(PREP_ROW_TILE, m)
    assert m % tm == 0
    h8 = tm // SUBLANES
    last8 = m // SUBLANES - 1
    full = lambda shape: pl.BlockSpec(shape, lambda i: tuple(0 for _ in shape))
    row_out = pl.BlockSpec((tm, C_R), lambda i: (i, 0))
    return pl.pallas_call(
        functools.partial(_prep_kernel, lp=lp, n_valid=n_valid),
        grid=(m // tm,),
        in_specs=[
            pl.BlockSpec((tm, RW_COLS), lambda i: (i, 0)),
            pl.BlockSpec((SUBLANES, RW_COLS), lambda i: (jnp.maximum(i * h8 - 1, 0), 0)),
            pl.BlockSpec((SUBLANES, RW_COLS), lambda i: (jnp.minimum((i + 1) * h8, last8), 0)),
            full((1, RW_COLS)), full((1, RW_COLS)),
            full((1, 2 * C_R)), full((2 * R_W, 2 * C_R)),
            full((1, 2 * C_R)), full((2 * R_A, 2 * C_R)),
            full((R_G, C_R)),
            full((1, C_R)), full((1, C_R)), full((1, C_R)),
        ],
        out_specs=[row_out] * 11,
        out_shape=[jax.ShapeDtypeStruct((m, C_R), F32 if name in ("lw_f", "lw_b") else BF16)
                   for name in ("r", "v", "an", "lw_f", "kd_f", "b_f", "lw_b", "kd_b", "b_b", "g", "bonus")],
        compiler_params=_cparams("parallel"),
        name="rwkv_prep",
    )(p_rw, p_rw, p_rw, mu_prev, mu_next, w0, w2, a0, a2, g2, k_k, k_a, r_k)


def _scan_kernel(r_ref, lw_ref, kd_ref, v_ref, an_ref, b_ref, y_ref,
                 s_ref, lhs_ref, tav_ref, arb_ref, arkv_ref, bt_ref, vtk_ref, wtot_ref, *, reverse):
    c = SCAN_CHUNK
    c2 = 2 * c
    tb = r_ref.shape[1]
    n_pairs = r_ref.shape[2] // LANES

    @pl.when(pl.program_id(1) == 0)
    def _():
        s_ref[...] = jnp.zeros_like(s_ref)

    def before(s, t, strict):
        if reverse:
            return (s > t) if strict else (s >= t)
        return (s < t) if strict else (s <= t)

    rc = lax.broadcasted_iota(jnp.int32, (c, c), 0)
    cc = lax.broadcasted_iota(jnp.int32, (c, c), 1)
    cum_mat = jnp.where(before(cc, rc, False), 1.0, 0.0).astype(BF16)
    r2 = lax.broadcasted_iota(jnp.int32, (c2, c2), 0)
    q2 = lax.broadcasted_iota(jnp.int32, (c2, c2), 1)
    same_head = (r2 >= c) == (q2 >= c)
    t2 = jnp.where(r2 >= c, r2 - c, r2)
    u2 = jnp.where(q2 >= c, q2 - c, q2)
    strict = same_head & before(u2, t2, True)
    incl = same_head & before(u2, t2, False)
    pair_mask = ((lax.broadcasted_iota(jnp.int32, (c2, LANES), 0) >= c)
                 == (lax.broadcasted_iota(jnp.int32, (c2, LANES), 1) >= N_R))
    last_row = 0 if reverse else c - 1

    def stack(x):
        return jnp.where(pair_mask, jnp.concatenate([x, x], axis=0), 0.0)

    eye = jnp.where(r2 == q2, 1.0, 0.0)
    steps = c.bit_length() - 1

    pairs = range(n_pairs)
    lane_slices = [slice(p * LANES, (p + 1) * LANES) for p in pairs]

    def prepare_chunk(ci, carry):
        rows = pl.ds(pl.multiple_of(ci * c, c), c)
        lw = lw_ref[0, rows, :]
        hi = lw.astype(BF16)
        rem = lw - hi.astype(F32)
        mid = rem.astype(BF16)
        lo = (rem - mid.astype(F32)).astype(BF16)
        cl = _bdot(cum_mat, hi) + _bdot(cum_mat, mid) + _bdot(cum_mat, lo)
        e_neg = jnp.exp(-cl)
        ab_all = an_ref[0, rows, :].astype(F32) * jnp.exp(cl - lw)
        rb_all = r_ref[0, rows, :].astype(F32) * jnp.exp(cl)
        bt_all = b_ref[0, rows, :].astype(F32) * e_neg
        kt_all = kd_ref[0, rows, :].astype(F32) * e_neg
        v_all = v_ref[0, rows, :].astype(F32)
        w_tot = jnp.exp(cl[last_row:last_row + 1, :])

        stacked = lambda x: [stack(x[:, ls]).astype(BF16) for ls in lane_slices]
        ab, rb, bt, kt, vs = stacked(ab_all), stacked(rb_all), stacked(bt_all), stacked(kt_all), stacked(v_all)
        gram = [_dot_nt(jnp.concatenate([ab[p], rb[p]], axis=0), jnp.concatenate([bt[p], kt[p]], axis=0))
                for p in pairs]
        n_ab = [jnp.where(strict, g[:c2, :c2], 0.0) for g in gram]
        a_ak = [jnp.where(strict, g[:c2, c2:], 0.0) for g in gram]
        a_rb = [jnp.where(incl, g[c2:, :c2], 0.0).astype(BF16) for g in gram]
        a_rk = [jnp.where(incl, g[c2:, c2:], 0.0) for g in gram]
        akv = [_bdot(jnp.concatenate([a_ak[p], a_rk[p]], axis=0).astype(BF16), vs[p]) for p in pairs]
        vtk = [_dot_tn(vs[p], kt[p]) for p in pairs]

        t_mat = [eye + n for n in n_ab]
        pw = [n.astype(BF16) for n in n_ab]
        for _ in range(1, steps):
            pw = [_bdot(x, x).astype(BF16) for x in pw]
            t_mat = [t_mat[p] + _bdot(t_mat[p].astype(BF16), pw[p]) for p in pairs]
        applied = [_bdot(t_mat[p].astype(BF16), jnp.concatenate([ab[p], akv[p][:c2].astype(BF16)], axis=1))
                   for p in pairs]
        for p in pairs:
            lhs_ref[ci, p] = jnp.concatenate([applied[p][:, :LANES].astype(BF16), rb[p]], axis=0)
            tav_ref[ci, p] = applied[p][:, LANES:]
            arb_ref[ci, p] = a_rb[p]
            arkv_ref[ci, p] = akv[p][c2:]
            bt_ref[ci, p] = bt[p]
            vtk_ref[ci, p] = vtk[p]
            wtot_ref[ci, p] = jnp.broadcast_to(w_tot[:, lane_slices[p]], (SUBLANES, LANES))
        return carry

    n_chunks = tb // c

    def advance_chunk(i, carry):
        ci = (n_chunks - 1 - i) if reverse else i
        rows = pl.ds(pl.multiple_of(ci * c, c), c)
        s0 = [s_ref[p] for p in pairs]
        xs = [_dot_nt(lhs_ref[ci, p], s0[p].astype(BF16)) for p in pairs]
        u = [(xs[p][:c2] + tav_ref[ci, p]).astype(BF16) for p in pairs]
        du = [_dot_tn(u[p], bt_ref[ci, p]) for p in pairs]
        yu = [_bdot(arb_ref[ci, p], u[p]) for p in pairs]
        for p in pairs:
            s_ref[p] = (s0[p] + du[p] + vtk_ref[ci, p]) * wtot_ref[ci, p][0:1, :]
            ys = xs[p][c2:] + yu[p] + arkv_ref[ci, p]
            y_ref[0, rows, lane_slices[p]] = ys[:c] + ys[c:]
        return carry

    lax.fori_loop(0, n_chunks, prepare_chunk, 0)
    lax.fori_loop(0, n_chunks, advance_chunk, 0)


def _wkv_scan(r, lw, kd, v, an, b, *, reverse, tb):
    bsz, lp, _ = r.shape
    nt = lp // tb
    assert lp % tb == 0 and tb % SCAN_CHUNK == 0
    n_chunks = tb // SCAN_CHUNK
    n_pairs = C_R // LANES
    c2 = 2 * SCAN_CHUNK
    tmap = (lambda bi, t: (bi, nt - 1 - t, 0)) if reverse else (lambda bi, t: (bi, t, 0))
    spec = pl.BlockSpec((1, tb, C_R), tmap)
    per_chunk = lambda rows, dtype: pltpu.VMEM((n_chunks, n_pairs, rows, LANES), dtype)
    return pl.pallas_call(
        functools.partial(_scan_kernel, reverse=reverse),
        grid=(bsz, nt),
        in_specs=[spec] * 6,
        out_specs=spec,
        out_shape=jax.ShapeDtypeStruct((bsz, lp, C_R), F32),
        scratch_shapes=[
            pltpu.VMEM((n_pairs, LANES, LANES), F32),
            per_chunk(2 * c2, BF16),
            per_chunk(c2, F32),
            per_chunk(c2, BF16),
            per_chunk(c2, F32),
            per_chunk(c2, BF16),
            per_chunk(LANES, F32),
            per_chunk(SUBLANES, F32),
        ],
        compiler_params=_cparams("parallel", "arbitrary"),
        name="wkv_scan_bwd" if reverse else "wkv_scan_fwd",
    )(r, lw, kd, v, an, b)


def _merge_kernel(yf_ref, yb_ref, bonus_ref, g_ref, o_ref, gate_ref, h_ref, lnw_ref, lnb_ref,
                  wrw_ref, watt_ref, wout_ref, out_ref):
    seg = _segment_ones(2 * LANES, 1.0 / N_R)
    y = yf_ref[...] + yb_ref[...]
    d = y - _segment_sum(y, seg)
    var = _segment_sum(d * d, seg)
    yn = d * lax.rsqrt(var + LNX_EPS) * lnw_ref[...] + lnb_ref[...]
    y_rw = _bdot(((yn + bonus_ref[...].astype(F32)) * g_ref[...].astype(F32)).astype(BF16), wrw_ref[...])
    y_att = _bdot(o_ref[...], watt_ref[...])
    gs = jax.nn.sigmoid(gate_ref[...])
    merged = gs[:, :D_MODEL] * y_att + gs[:, D_MODEL:] * y_rw
    out_ref[...] = h_ref[...] + _bdot(merged.astype(BF16), wout_ref[...])


def _merge(yf, yb, bonus, g, o, gate, h, lnw, lnb, wrw, watt, wout):
    m = h.shape[0]
    tm = min(PREP_ROW_TILE, m)
    assert m % tm == 0
    rows = lambda n: pl.BlockSpec((tm, n), lambda i: (i, 0))
    full = lambda shape: pl.BlockSpec(shape, lambda i: tuple(0 for _ in shape))
    return pl.pallas_call(
        _merge_kernel,
        grid=(m // tm,),
        in_specs=[rows(C_R), rows(C_R), rows(C_R), rows(C_R), rows(ATT_V), rows(GATE_COLS), rows(D_MODEL),
                  full((1, C_R)), full((1, C_R)),
                  full((C_R, D_MODEL)), full((ATT_V, D_MODEL)), full((D_MODEL, D_MODEL))],
        out_specs=rows(D_MODEL),
        out_shape=jax.ShapeDtypeStruct((m, D_MODEL), F32),
        compiler_params=_cparams("parallel"),
        name="merge",
    )(yf, yb, bonus, g, o, gate, h, lnw, lnb, wrw, watt, wout)


def _pair_block_diag(w):
    z = jnp.zeros_like(w[0])
    return jnp.concatenate([jnp.concatenate([w[0], z], axis=1), jnp.concatenate([z, w[1]], axis=1)], axis=0)


def _scan_time_block(lp):
    for tb in (512, 384, 256, 128):
        if lp % tb == 0:
            return tb
    raise ValueError(lp)


def _encode(x, P, bias_tiles):
    b, seq, _ = x.shape
    n_valid = N_META + seq
    lp = (n_valid // SEQ_ALIGN + 1) * SEQ_ALIGN
    m = b * lp
    meta = jnp.broadcast_to(P["meta_tokens"].astype(x.dtype)[None], (b, N_META, D_MODEL))
    h0 = jnp.concatenate([meta, x, jnp.zeros((b, lp - n_valid, D_MODEL), x.dtype)], axis=1).reshape(m, D_MODEL)

    row = lambda a: a.reshape(1, -1)
    bf = lambda a: a.astype(BF16)
    l = 0
    h1 = _ffn(h0, row(P["ffn1_norm"][l]), bf(P["ffn1_w_gate"][l]), bf(P["ffn1_w_up"][l]), bf(P["ffn1_w_down"][l]),
              row(P["final_norm"]), final_norm=False)

    w_in = bf(P["w_in"][l])
    g_mix = row(P["mix_norm"][l])
    qkv = _proj(h1, g_mix, w_in[:, :ATT_COLS], BF16, q_cols=ATT_QK, name="proj_qkv")
    p_rw = _proj(h1, g_mix, w_in[:, ATT_COLS:ATT_COLS + RW_COLS], F32, name="proj_rw")
    p_gate = _proj(h1, g_mix, w_in[:, ATT_COLS + RW_COLS:], F32, name="proj_gate")

    lam_init = 0.8 - 0.6 * math.exp(-0.3 * l)
    o_att = _attention(qkv.reshape(b, lp, ATT_COLS), bias_tiles, P["attn_subln"][l].reshape(DV_A, 1),
                       row(P["attn_lambda_q1"][l]), row(P["attn_lambda_k1"][l]),
                       row(P["attn_lambda_q2"][l]), row(P["attn_lambda_k2"][l]),
                       n_valid=n_valid, tq=Q_TILE, lam_init=lam_init).reshape(m, ATT_V)

    (r, v, an, lw_f, kd_f, b_f, lw_b, kd_b, b_b, g, bonus) = _rwkv_prep(
        p_rw, row(P["rw_mu_prev"][l]), row(P["rw_mu_next"][l]),
        row(P["rw_w0"][l]), bf(_pair_block_diag(P["rw_w2"][l])),
        row(P["rw_a0"][l]), bf(_pair_block_diag(P["rw_a2"][l])),
        bf(P["rw_g2"][l]), row(P["rw_k_k"][l]), row(P["rw_k_a"][l]), row(P["rw_r_k"][l]),
        lp=lp, n_valid=n_valid)
    seq3 = lambda a: a.reshape(b, lp, C_R)
    tb = _scan_time_block(lp)
    y_f = _wkv_scan(seq3(r), seq3(lw_f), seq3(kd_f), seq3(v), seq3(an), seq3(b_f), reverse=False, tb=tb)
    y_b = _wkv_scan(seq3(r), seq3(lw_b), seq3(kd_b), seq3(v), seq3(an), seq3(b_b), reverse=True, tb=tb)

    h2 = _merge(y_f.reshape(m, C_R), y_b.reshape(m, C_R), bonus, g, o_att, p_gate, h1,
                row(P["rw_lnx_w"][l]), row(P["rw_lnx_b"][l]),
                bf(P["w_rw_branch"][l]), bf(P["w_attn_branch"][l]), bf(P["w_out"][l]))
    out = _ffn(h2, row(P["ffn2_norm"][l]), bf(P["ffn2_w_gate"][l]), bf(P["ffn2_w_up"][l]), bf(P["ffn2_w_down"][l]),
               row(P["final_norm"]), final_norm=True)
    return out.reshape(b, lp, D_MODEL)[:, N_META:n_valid]


def kernel(x_prompt, x_sample, meta_tokens, rel_bias, ffn1_norm, ffn1_w_gate, ffn1_w_up, ffn1_w_down,
           mix_norm, w_in, attn_lambda_q1, attn_lambda_k1, attn_lambda_q2, attn_lambda_k2, attn_subln,
           w_attn_branch, rw_mu_prev, rw_mu_next, rw_w0, rw_w2, rw_a0, rw_a2, rw_g2, rw_k_k, rw_k_a,
           rw_r_k, rw_lnx_w, rw_lnx_b, w_rw_branch, w_out, ffn2_norm, ffn2_w_gate, ffn2_w_up,
           ffn2_w_down, final_norm):
    assert ffn1_norm.shape[0] == 1, "single layer"
    P = dict(meta_tokens=meta_tokens, ffn1_norm=ffn1_norm, ffn1_w_gate=ffn1_w_gate,
             ffn1_w_up=ffn1_w_up, ffn1_w_down=ffn1_w_down, mix_norm=mix_norm, w_in=w_in,
             attn_lambda_q1=attn_lambda_q1, attn_lambda_k1=attn_lambda_k1, attn_lambda_q2=attn_lambda_q2,
             attn_lambda_k2=attn_lambda_k2, attn_subln=attn_subln, w_attn_branch=w_attn_branch,
             rw_mu_prev=rw_mu_prev, rw_mu_next=rw_mu_next, rw_w0=rw_w0, rw_w2=rw_w2, rw_a0=rw_a0,
             rw_a2=rw_a2, rw_g2=rw_g2, rw_k_k=rw_k_k, rw_k_a=rw_k_a, rw_r_k=rw_r_k, rw_lnx_w=rw_lnx_w,
             rw_lnx_b=rw_lnx_b, w_rw_branch=w_rw_branch, w_out=w_out, ffn2_norm=ffn2_norm,
             ffn2_w_gate=ffn2_w_gate, ffn2_w_up=ffn2_w_up, ffn2_w_down=ffn2_w_down, final_norm=final_norm)
    bias_tiles = _bias_tiles(rel_bias, Q_TILE)
    return (_encode(x_prompt, P, bias_tiles), _encode(x_sample, P, bias_tiles))
```
